```python
import math
import jax, jax.numpy as jnp
from jax import lax
import numpy as np

D_MODEL = 1024
BATCH = 8
SEQ = 2048
DEPTH = 1
DEC_BATCH = 128
DEC_SEQ = 8
PAST_LEN = 16384
PAGE_SIZE = 128

N_META = 16
SC_WIDTH = D_MODEL
SC_TAPS = 3
GDN_HEADS = 8
GDN_DK = 128
GDN_DV = 128
GDN_KEY = GDN_HEADS * GDN_DK
GDN_VAL = GDN_HEADS * GDN_DV
GDN_QKV = 2 * GDN_KEY + GDN_VAL
GDN_TAPS = 4
GDN_CHUNK = 64
D_FF = 2816
IN_COLS = 3 * SC_WIDTH + GDN_QKV + 2 * GDN_HEADS + GDN_VAL + 2 * D_MODEL
EPS = 1e-6
L2_EPS = 1e-6

kernel_name = 'macaron_gated_shortconv_deltanet_step'


def rmsnorm(x, g):
    xf = x.astype(jnp.float32)
    y = xf * lax.rsqrt(jnp.mean(xf * xf, axis=-1, keepdims=True) + EPS)
    return (y * g.astype(jnp.float32)).astype(x.dtype)


def l2norm(x):
    return x * lax.rsqrt(jnp.sum(x * x, axis=-1, keepdims=True) + L2_EPS)


def swiglu(x, w_gate, w_up, w_down):
    return (jax.nn.silu(x @ w_gate) * (x @ w_up)) @ w_down


def causal_dwconv(x, buf, w):
    taps = w.shape[0]
    l = x.shape[1]
    xp = jnp.concatenate([buf.astype(x.dtype), x], axis=1)
    y = xp[:, 0:l] * w[0]
    for j in range(1, taps):
        y = y + xp[:, j:j + l] * w[j]
    return y, xp[:, l:]


def split_columns(proj):
    sizes = (SC_WIDTH, SC_WIDTH, SC_WIDTH, GDN_QKV, GDN_HEADS, GDN_HEADS, GDN_VAL, D_MODEL, D_MODEL)
    offs = np.cumsum(sizes)[:-1].tolist()
    return jnp.split(proj, offs, axis=-1)


def gdn_chunked(q, k, v, log_a, beta, s0, chunk):
    b, l, h, dk = q.shape
    dv = v.shape[-1]
    n = l // chunk

    def blk(t):
        t = t.reshape((b, n, chunk, h) + t.shape[3:])
        return jnp.moveaxis(t, 3, 2)

    q, k, v, log_a, beta = blk(q), blk(k), blk(v), blk(log_a), blk(beta)
    g = jnp.cumsum(log_a, axis=-1)
    diff = g[..., :, None] - g[..., None, :]
    idx = jnp.arange(chunk)
    strict = idx[:, None] > idx[None, :]
    incl = idx[:, None] >= idx[None, :]
    dec_strict = jnp.exp(jnp.where(strict, diff, -jnp.inf))
    dec_incl = jnp.exp(jnp.where(incl, diff, -jnp.inf))
    a_kk = beta[..., :, None] * jnp.einsum('bnhid,bnhjd->bnhij', k, k) * dec_strict
    rhs = jnp.concatenate([beta[..., None] * v, (beta * jnp.exp(g))[..., None] * k], axis=-1)
    sol = lax.linalg.triangular_solve(a_kk, rhs, left_side=True, lower=True, unit_diagonal=True)
    u, w = sol[..., :dv], sol[..., dv:]
    a_qk = jnp.einsum('bnhid,bnhjd->bnhij', q, k) * dec_incl
    q_dec = q * jnp.exp(g)[..., None]
    g_last = g[..., -1]
    k_dec = k * jnp.exp(g_last[..., None] - g)[..., None]

    def step(s, xs):
        u_c, w_c, aqk_c, qd_c, kd_c, gl_c = xs
        nu = u_c - jnp.einsum('bhcd,bhde->bhce', w_c, s)
        o = jnp.einsum('bhcd,bhde->bhce', qd_c, s) + jnp.einsum('bhij,bhje->bhie', aqk_c, nu)
        s = jnp.exp(gl_c)[..., None, None] * s + jnp.einsum('bhcd,bhce->bhde', kd_c, nu)
        return s, o

    xs = tuple(jnp.moveaxis(t, 1, 0) for t in (u, w, a_qk, q_dec, k_dec, g_last))
    s_fin, o = lax.scan(step, s0, xs)
    o = jnp.transpose(o, (1, 0, 3, 2, 4)).reshape(b, l, h, dv)
    return o, s_fin


def gdn_mixer(qkv_pre, a_raw, b_raw, z, conv_buf, s0, conv_w, a_log, dt_bias, norm_g, segments):
    bsz, l, _ = qkv_pre.shape
    f32 = jnp.float32
    qkv, conv_buf = causal_dwconv(qkv_pre, conv_buf, conv_w)
    qkv = jax.nn.silu(qkv).astype(f32)
    q, k, v = jnp.split(qkv, [GDN_KEY, 2 * GDN_KEY], axis=-1)
    q = l2norm(q.reshape(bsz, l, GDN_HEADS, GDN_DK)) * (GDN_DK ** -0.5)
    k = l2norm(k.reshape(bsz, l, GDN_HEADS, GDN_DK))
    v = v.reshape(bsz, l, GDN_HEADS, GDN_DV)
    beta = jax.nn.sigmoid(b_raw.astype(f32))
    log_a = -jnp.exp(a_log.astype(f32)) * jax.nn.softplus(a_raw.astype(f32) + dt_bias.astype(f32))
    s = s0.astype(f32)
    outs = []
    start = 0
    for seg_len, chunk in segments:
        sl = slice(start, start + seg_len)
        o_seg, s = gdn_chunked(q[:, sl], k[:, sl], v[:, sl], log_a[:, sl], beta[:, sl], s, chunk)
        outs.append(o_seg)
        start += seg_len
    o = jnp.concatenate(outs, axis=1)
    zh = z.astype(f32).reshape(bsz, l, GDN_HEADS, GDN_DV)
    o = rmsnorm(o, norm_g) * jax.nn.silu(zh)
    return o.reshape(bsz, l, GDN_VAL).astype(qkv_pre.dtype), conv_buf, s.astype(s0.dtype)


def layer(x, sc_buf, gdn_buf, gdn_s, segments, w):
    (n_f1, f1_g, f1_u, f1_d, n_mix, w_in, sc_w, gc_w, a_log, dt_bias, g_norm,
     w_a, w_b, w_o, n_f2, f2_g, f2_u, f2_d) = w
    x = x + 0.5 * swiglu(rmsnorm(x, n_f1), f1_g, f1_u, f1_d)
    h = rmsnorm(x, n_mix)
    proj = h @ w_in
    sc_b, sc_c, sc_x, qkv_pre, a_raw, b_raw, z, gate_a, gate_b = split_columns(proj)
    conv_out, sc_buf = causal_dwconv(sc_c * sc_x, sc_buf, sc_w)
    y_a = sc_b * conv_out
    y_b, gdn_buf, gdn_s = gdn_mixer(qkv_pre, a_raw, b_raw, z, gdn_buf, gdn_s, gc_w,
                                    a_log, dt_bias, g_norm, segments)
    merged = jax.nn.sigmoid(gate_a) * (y_a @ w_a) + jax.nn.sigmoid(gate_b) * (y_b @ w_b)
    x = x + merged @ w_o
    x = x + 0.5 * swiglu(rmsnorm(x, n_f2), f2_g, f2_u, f2_d)
    return x, sc_buf, gdn_buf, gdn_s


def trunk(x, sc_bufs, gdn_bufs, gdn_states, segments, weights, norm_final):
    new_sc, new_gc, new_gs = [], [], []
    for i in range(DEPTH):
        x, sc, gc, gs = layer(x, sc_bufs[i], gdn_bufs[i], gdn_states[i], segments,
                              tuple(wt[i] for wt in weights))
        new_sc.append(sc)
        new_gc.append(gc)
        new_gs.append(gs)
    return rmsnorm(x, norm_final), jnp.stack(new_sc), jnp.stack(new_gc), jnp.stack(new_gs)


def setup_inputs(seed: int = 0) -> dict:
    key = jax.random.key(seed)
    ks = jax.random.split(key, 32)
    f32 = jnp.float32

    def nrm(k, shape, scale):
        return jax.random.normal(k, shape, f32) * scale

    def gain(k, shape):
        return 1.0 + 0.02 * jax.random.normal(k, shape, f32)

    dt = jnp.exp(jax.random.uniform(ks[14], (DEPTH, GDN_HEADS), f32, math.log(1e-3), math.log(1e-1)))
    dt_bias = dt + jnp.log(-jnp.expm1(-dt))
    a_log = jnp.log(jax.random.uniform(ks[13], (DEPTH, GDN_HEADS), f32, 1.0, 16.0))
    return {
        'x_prompt': nrm(ks[0], (BATCH, SEQ, D_MODEL), 1.0),
        'x_sample': nrm(ks[1], (DEC_BATCH, DEC_SEQ, D_MODEL), 1.0),
        'state_sconv': nrm(ks[2], (DEPTH, DEC_BATCH, SC_TAPS - 1, SC_WIDTH), 1.0),
        'state_gdn_conv': nrm(ks[3], (DEPTH, DEC_BATCH, GDN_TAPS - 1, GDN_QKV), 1.0),
        'state_gdn': nrm(ks[4], (DEPTH, DEC_BATCH, GDN_HEADS, GDN_DK, GDN_DV), 0.1),
        'meta_tokens': nrm(ks[5], (N_META, D_MODEL), 1.0),
        'norm_ffn1': gain(ks[6], (DEPTH, D_MODEL)),
        'ffn1_w_gate': nrm(ks[7], (DEPTH, D_MODEL, D_FF), D_MODEL ** -0.5),
        'ffn1_w_up': nrm(ks[8], (DEPTH, D_MODEL, D_FF), D_MODEL ** -0.5),
        'ffn1_w_down': nrm(ks[9], (DEPTH, D_FF, D_MODEL), D_FF ** -0.5),
        'norm_mix': gain(ks[10], (DEPTH, D_MODEL)),
        'w_in': nrm(ks[11], (DEPTH, D_MODEL, IN_COLS), D_MODEL ** -0.5),
        'sconv_w': nrm(ks[12], (DEPTH, SC_TAPS, SC_WIDTH), SC_TAPS ** -0.5),
        'gdn_conv_w': nrm(ks[15], (DEPTH, GDN_TAPS, GDN_QKV), GDN_TAPS ** -0.5),
        'gdn_a_log': a_log,
        'gdn_dt_bias': dt_bias,
        'gdn_norm': gain(ks[16], (DEPTH, GDN_DV)),
        'w_a_out': nrm(ks[17], (DEPTH, SC_WIDTH, D_MODEL), SC_WIDTH ** -0.5),
        'w_b_out': nrm(ks[18], (DEPTH, GDN_VAL, D_MODEL), GDN_VAL ** -0.5),
        'w_o': nrm(ks[19], (DEPTH, D_MODEL, D_MODEL), D_MODEL ** -0.5),
        'norm_ffn2': gain(ks[20], (DEPTH, D_MODEL)),
        'ffn2_w_gate': nrm(ks[21], (DEPTH, D_MODEL, D_FF), D_MODEL ** -0.5),
        'ffn2_w_up': nrm(ks[22], (DEPTH, D_MODEL, D_FF), D_MODEL ** -0.5),
        'ffn2_w_down': nrm(ks[23], (DEPTH, D_FF, D_MODEL), D_FF ** -0.5),
        'norm_final': gain(ks[24], (D_MODEL,)),
    }


def reference(x_prompt, x_sample, state_sconv, state_gdn_conv, state_gdn, meta_tokens,
              norm_ffn1, ffn1_w_gate, ffn1_w_up, ffn1_w_down, norm_mix, w_in, sconv_w,
              gdn_conv_w, gdn_a_log, gdn_dt_bias, gdn_norm, w_a_out, w_b_out, w_o,
              norm_ffn2, ffn2_w_gate, ffn2_w_up, ffn2_w_down, norm_final):
    weights = (norm_ffn1, ffn1_w_gate, ffn1_w_up, ffn1_w_down, norm_mix, w_in, sconv_w,
               gdn_conv_w, gdn_a_log, gdn_dt_bias, gdn_norm, w_a_out, w_b_out, w_o,
               norm_ffn2, ffn2_w_gate, ffn2_w_up, ffn2_w_down)
    bsz, seq = x_prompt.shape[0], x_prompt.shape[1]
    dt = x_prompt.dtype
    meta = jnp.broadcast_to(meta_tokens.astype(dt)[None], (bsz, N_META, D_MODEL))
    xp = jnp.concatenate([meta, x_prompt], axis=1)
    zero_sc = jnp.zeros((DEPTH, bsz, SC_TAPS - 1, SC_WIDTH), dt)
    zero_gc = jnp.zeros((DEPTH, bsz, GDN_TAPS - 1, GDN_QKV), dt)
    zero_gs = jnp.zeros((DEPTH, bsz, GDN_HEADS, GDN_DK, GDN_DV), dt)
    seg_prompt = ((N_META, N_META), (seq, GDN_CHUNK))
    yp, p_sc, p_gc, p_gs = trunk(xp, zero_sc, zero_gc, zero_gs, seg_prompt, weights, norm_final)
    y_prompt = yp[:, N_META:]
    dec_len = x_sample.shape[1]
    seg_sample = ((dec_len, dec_len),)
    y_sample, s_sc, s_gc, s_gs = trunk(x_sample, state_sconv, state_gdn_conv, state_gdn,
                                       seg_sample, weights, norm_final)
    return (y_prompt, y_sample, p_sc, p_gc, p_gs, s_sc, s_gc, s_gs)
```

```python
import functools

import jax
import jax.numpy as jnp
from jax import lax
from jax.experimental import pallas as pl
from jax.experimental.pallas import tpu as pltpu

D_MODEL = 1024
D_FF = 2816
N_META = 16
SC_TAPS = 3
GDN_HEADS = 8
GDN_DK = 128
GDN_DV = 128
GDN_QKV = 3 * GDN_HEADS * GDN_DK
GDN_TAPS = 4
EPS = 1e-6
L2_EPS = 1e-6

LANES = 128
SUBLANES = 8
FF_CHUNK = 256
VMEM_LIMIT = 52 * 1024 * 1024

F32 = jnp.float32
BF16 = jnp.bfloat16


def _sigmoid(x):
    return 1.0 / (1.0 + jnp.exp(-x))


def _rms(x, g):
    return x * lax.rsqrt(jnp.mean(x * x, axis=-1, keepdims=True) + EPS) * g


def _dot(a, b):
    return jnp.dot(a, b, preferred_element_type=F32)


def _dot_nt(a, b):
    return lax.dot_general(a, b, (((1,), (1,)), ((), ())), preferred_element_type=F32)


def _dot_tn(a, b):
    return lax.dot_general(a, b, (((0,), (0,)), ((), ())), preferred_element_type=F32)


def _split2(x):
    hi = x.astype(BF16)
    lo = (x - hi.astype(F32)).astype(BF16)
    return hi, lo


def _split3(x):
    hi = x.astype(BF16)
    r = x - hi.astype(F32)
    mid = r.astype(BF16)
    lo = (r - mid.astype(F32)).astype(BF16)
    return hi, mid, lo


def _dot_hi(a, b):
    a_hi, a_lo = _split2(a)
    b_hi, b_lo = _split2(b)
    return _dot(a_hi, b_hi) + (_dot(a_hi, b_lo) + _dot(a_lo, b_hi))


def _dot_exact_lhs(m_bf16, x):
    hi, mid, lo = _split3(x)
    return _dot(m_bf16, hi) + (_dot(m_bf16, mid) + _dot(m_bf16, lo))


def _resident(shape):
    nd = len(shape)
    return pl.BlockSpec(shape, lambda *_: (0,) * nd, pipeline_mode=pl.Buffered(1))


def _ffn_kernel(x_ref, ng_ref, wg_ref, wu_ref, wd_ref, nn_ref, *rest, final):
    if final:
        y_ref, acc_ref = rest
    else:
        x1_ref, h_ref, acc_ref = rest
    x = x_ref[...]
    h = _rms(x, ng_ref[...]).astype(BF16)
    acc_ref[...] = jnp.zeros_like(acc_ref)

    def chunk(f, carry):
        off = pl.multiple_of(f * FF_CHUNK, FF_CHUNK)
        g = _dot(h, wg_ref[:, pl.ds(off, FF_CHUNK)])
        u = _dot(h, wu_ref[:, pl.ds(off, FF_CHUNK)])
        a = (g * _sigmoid(g) * u).astype(BF16)
        acc_ref[...] += _dot(a, wd_ref[pl.ds(off, FF_CHUNK), :])
        return carry

    lax.fori_loop(0, D_FF // FF_CHUNK, chunk, 0)
    x1 = x + 0.5 * acc_ref[...]
    if final:
        y_ref[...] = _rms(x1, nn_ref[...])
    else:
        x1_ref[...] = x1
        h_ref[...] = _rms(x1, nn_ref[...]).astype(BF16)


def _ffn(x, ng, wg, wu, wd, nn, *, final, tm):
    m = x.shape[0]
    tile = pl.BlockSpec((tm, D_MODEL), lambda i: (i, 0))
    if final:
        out_shape = jax.ShapeDtypeStruct((m, D_MODEL), F32)
        out_specs = tile
    else:
        out_shape = (jax.ShapeDtypeStruct((m, D_MODEL), F32),
                     jax.ShapeDtypeStruct((m, D_MODEL), BF16))
        out_specs = (tile, tile)
    return pl.pallas_call(
        functools.partial(_ffn_kernel, final=final),
        out_shape=out_shape,
        grid=(m // tm,),
        in_specs=[tile, _resident((1, D_MODEL)), _resident((D_MODEL, D_FF)),
                  _resident((D_MODEL, D_FF)), _resident((D_FF, D_MODEL)),
                  _resident((1, D_MODEL))],
        out_specs=out_specs,
        scratch_shapes=[pltpu.VMEM((tm, D_MODEL), F32)],
        compiler_params=pltpu.CompilerParams(
            dimension_semantics=("arbitrary",), vmem_limit_bytes=VMEM_LIMIT),
        name="ffn_final" if final else "ffn_mid",
    )(x, ng, wg, wu, wd, nn)


N_PROJ_BLOCKS = 9


def _proj_kernel(h_ref, w_ref, wab_ref, p_ref, ab_ref):
    j = pl.program_id(1)
    h = h_ref[...]
    off = pl.multiple_of(j * D_MODEL, D_MODEL)
    p_ref[...] = _dot(h, w_ref[:, pl.ds(off, D_MODEL)])

    @pl.when(j == 0)
    def _():
        ab_ref[...] = _dot(h, wab_ref[...])


def _proj(h, w_main, w_ab, *, tm):
    m = h.shape[0]
    return pl.pallas_call(
        _proj_kernel,
        out_shape=(jax.ShapeDtypeStruct((m, N_PROJ_BLOCKS * D_MODEL), F32),
                   jax.ShapeDtypeStruct((m, LANES), F32)),
        grid=(m // tm, N_PROJ_BLOCKS),
        in_specs=[pl.BlockSpec((tm, D_MODEL), lambda i, j: (i, 0)),
                  _resident((D_MODEL, N_PROJ_BLOCKS * D_MODEL)),
                  _resident((D_MODEL, LANES))],
        out_specs=(pl.BlockSpec((tm, D_MODEL), lambda i, j: (i, j)),
                   pl.BlockSpec((tm, LANES), lambda i, j: (i, 0))),
        compiler_params=pltpu.CompilerParams(
            dimension_semantics=("arbitrary", "arbitrary"), vmem_limit_bytes=VMEM_LIMIT),
        name="proj",
    )(h, w_main, w_ab)


HIST = SUBLANES


def _mixer_kernel(scb_ref, scc_ref, scx_ref, qkv_ref, z_ref, ab_ref,
                  sc0_ref, gc0_ref, s0_ref,
                  scw_ref, gcw_ref, alog_ref, dtb_ref, gn_ref,
                  ya_ref, yb_ref, sco_ref, gco_ref, so_ref,
                  ext_sc, ext_gc, *, G, C):
    R = G * C
    c = pl.program_id(1)

    @pl.when(c == 0)
    def _():
        ext_sc[:, HIST - (SC_TAPS - 1):HIST, :] = sc0_ref[...]
        ext_gc[:, HIST - (GDN_TAPS - 1):HIST, :] = gc0_ref[...]
        so_ref[...] = s0_ref[...]

    cx = scc_ref[...] * scx_ref[...]
    ext_sc[:, HIST:HIST + C, :] = cx.reshape(G, C, D_MODEL)
    conv = None
    for j in range(SC_TAPS):
        lo = HIST - (SC_TAPS - 1) + j
        term = ext_sc[:, lo:lo + C, :] * scw_ref[j:j + 1, :]
        conv = term if conv is None else conv + term
    ya_ref[...] = (scb_ref[...] * conv.reshape(R, D_MODEL)).astype(BF16)
    new_sc = ext_sc[:, HIST + C - (SC_TAPS - 1):HIST + C, :]
    ext_sc[:, HIST - (SC_TAPS - 1):HIST, :] = new_sc
    sco_ref[...] = new_sc

    ext_gc[:, HIST:HIST + C, :] = qkv_ref[...].reshape(G, C, GDN_QKV)
    conv = None
    for j in range(GDN_TAPS):
        lo = HIST - (GDN_TAPS - 1) + j
        term = ext_gc[:, lo:lo + C, :] * gcw_ref[j:j + 1, :]
        conv = term if conv is None else conv + term
    new_gc = ext_gc[:, HIST + C - (GDN_TAPS - 1):HIST + C, :]
    ext_gc[:, HIST - (GDN_TAPS - 1):HIST, :] = new_gc
    gco_ref[...] = new_gc
    conv = conv.reshape(R, GDN_QKV)
    qkv = conv * _sigmoid(conv)

    ab = ab_ref[...]
    xa = ab + dtb_ref[...]
    softplus = jnp.maximum(xa, 0.0) + jnp.log1p(jnp.exp(-jnp.abs(xa)))
    log_a = -jnp.exp(alog_ref[...]) * softplus
    beta_all = _sigmoid(ab)

    row = lax.broadcasted_iota(jnp.int32, (R, R), 0)
    col = lax.broadcasted_iota(jnp.int32, (R, R), 1)
    if G == 1:
        same = None
        incl = row >= col
        strict = row > col
        seg_all = jnp.ones((R, R), BF16)
    else:
        same = (row // C) == (col // C)
        incl = jnp.logical_and(same, row >= col)
        strict = jnp.logical_and(same, row > col)
        seg_all = jnp.where(same, 1.0, 0.0).astype(BF16)
    seg_tri = jnp.where(incl, 1.0, 0.0).astype(BF16)
    eye = jnp.where(row == col, 1.0, 0.0).astype(F32)

    g_col = _dot_exact_lhs(seg_tri, log_a)
    g_last = _dot_exact_lhs(seg_all, log_a)
    g_row = g_col.T
    exp_g = jnp.exp(g_col)
    exp_rest = jnp.exp(g_last - g_col)
    exp_last = jnp.exp(g_last)

    n_sq = max(C.bit_length() - 2, 0)

    for h in range(GDN_HEADS):
        hs = slice(h * GDN_DK, (h + 1) * GDN_DK)
        q_h = qkv[:, h * GDN_DK:(h + 1) * GDN_DK]
        k_h = qkv[:, GDN_HEADS * GDN_DK + h * GDN_DK:GDN_HEADS * GDN_DK + (h + 1) * GDN_DK]
        v_h = qkv[:, 2 * GDN_HEADS * GDN_DK + h * GDN_DV:2 * GDN_HEADS * GDN_DK + (h + 1) * GDN_DV]
        q_h = q_h * (lax.rsqrt(jnp.sum(q_h * q_h, axis=-1, keepdims=True) + L2_EPS)
                     * (GDN_DK ** -0.5))
        k_h = k_h * lax.rsqrt(jnp.sum(k_h * k_h, axis=-1, keepdims=True) + L2_EPS)

        gc = g_col[:, h:h + 1]
        gr = g_row[h:h + 1, :]
        be = beta_all[:, GDN_HEADS + h:GDN_HEADS + h + 1]
        eg = exp_g[:, h:h + 1]
        er = exp_rest[:, h:h + 1]

        dec = jnp.exp(jnp.minimum(gc - gr, 0.0))
        k_b = k_h.astype(BF16)
        kk = _dot_nt(k_b, k_b)
        qk = _dot_nt(q_h.astype(BF16), k_b)
        a_kk = jnp.where(strict, be * kk * dec, 0.0)
        a_qk = jnp.where(incl, qk * dec, 0.0)

        p = -a_kk
        t = eye + p
        for _ in range(n_sq):
            p = _dot_hi(p, p)
            t = t + _dot_hi(t, p)
        rhs = jnp.concatenate([be * v_h, (be * eg) * k_h], axis=-1)
        sol = _dot_hi(t, rhs)
        u = sol[:, :GDN_DV]
        w = sol[:, GDN_DV:]
        q_dec = q_h * eg
        k_dec = (k_h * er).astype(BF16)

        nus, qss = [], []
        for g in range(G):
            rs = slice(g * C, (g + 1) * C)
            s = so_ref[g, h]
            lhs = jnp.concatenate([w[rs], q_dec[rs]], axis=0).astype(BF16)
            r = _dot(lhs, s.astype(BF16))
            nus.append(u[rs] - r[:C])
            qss.append(r[C:])
        nu = nus[0] if G == 1 else jnp.concatenate(nus, axis=0)
        qs = qss[0] if G == 1 else jnp.concatenate(qss, axis=0)
        nu_b = nu.astype(BF16)
        o = qs + _dot(a_qk.astype(BF16), nu_b)
        for g in range(G):
            rs = slice(g * C, (g + 1) * C)
            s = so_ref[g, h]
            el = exp_last[g * C:g * C + 1, h:h + 1]
            so_ref[g, h] = el * s + _dot_tn(k_dec[rs], nu_b[rs])

        on = o * lax.rsqrt(jnp.mean(o * o, axis=-1, keepdims=True) + EPS) * gn_ref[...]
        z_h = z_ref[:, hs]
        yb_ref[:, hs] = (on * (z_h * _sigmoid(z_h))).astype(BF16)


def _mixer(proj, ab, sc0, gc0, s0, scw, gcw, alog, dtb, gn, *, G, C, NC):
    R = G * C
    m = proj.shape[0]
    nseq = sc0.shape[0]
    nb = nseq // G
    assert m == nb * NC * R
    rows = lambda i, c: i * NC + c

    def col(blk, width=D_MODEL):
        return pl.BlockSpec((R, width), lambda i, c: (rows(i, c), blk))

    seq3 = lambda shape: pl.BlockSpec((G,) + shape, lambda i, c: (i, 0, 0))
    state = pl.BlockSpec((G, GDN_HEADS, GDN_DK, GDN_DV), lambda i, c: (i, 0, 0, 0))
    return pl.pallas_call(
        functools.partial(_mixer_kernel, G=G, C=C),
        out_shape=(jax.ShapeDtypeStruct((m, D_MODEL), BF16),
                   jax.ShapeDtypeStruct((m, D_MODEL), BF16),
                   jax.ShapeDtypeStruct((nseq, SC_TAPS - 1, D_MODEL), F32),
                   jax.ShapeDtypeStruct((nseq, GDN_TAPS - 1, GDN_QKV), F32),
                   jax.ShapeDtypeStruct((nseq, GDN_HEADS, GDN_DK, GDN_DV), F32)),
        grid=(nb, NC),
        in_specs=[col(0), col(1), col(2), col(1, GDN_QKV), col(6),
                  pl.BlockSpec((R, LANES), lambda i, c: (rows(i, c), 0)),
                  seq3((SC_TAPS - 1, D_MODEL)), seq3((GDN_TAPS - 1, GDN_QKV)), state,
                  _resident((SC_TAPS, D_MODEL)), _resident((GDN_TAPS, GDN_QKV)),
                  _resident((1, LANES)), _resident((1, LANES)), _resident((1, GDN_DV))],
        out_specs=(col(0), col(0),
                   seq3((SC_TAPS - 1, D_MODEL)), seq3((GDN_TAPS - 1, GDN_QKV)), state),
        scratch_shapes=[pltpu.VMEM((G, HIST + C, D_MODEL), F32),
                        pltpu.VMEM((G, HIST + C, GDN_QKV), F32)],
        compiler_params=pltpu.CompilerParams(
            dimension_semantics=("arbitrary", "arbitrary"), vmem_limit_bytes=VMEM_LIMIT),
        name=f"mixer_g{G}_c{C}",
    )(proj, proj, proj, proj, proj, ab, sc0, gc0, s0, scw, gcw, alog, dtb, gn)


def _merge_kernel(ya_ref, yb_ref, ga_ref, gb_ref, x1_ref, wa_ref, wb_ref, wo_ref, x2_ref):
    merged = (_sigmoid(ga_ref[...]) * _dot(ya_ref[...], wa_ref[...])
              + _sigmoid(gb_ref[...]) * _dot(yb_ref[...], wb_ref[...]))
    x2_ref[...] = x1_ref[...] + _dot(merged.astype(BF16), wo_ref[...])


def _merge(ya, yb, proj, x1, wa, wb, wo, *, tm):
    m = x1.shape[0]
    tile = pl.BlockSpec((tm, D_MODEL), lambda i: (i, 0))
    return pl.pallas_call(
        _merge_kernel,
        out_shape=jax.ShapeDtypeStruct((m, D_MODEL), F32),
        grid=(m // tm,),
        in_specs=[tile, tile,
                  pl.BlockSpec((tm, D_MODEL), lambda i: (i, 7)),
                  pl.BlockSpec((tm, D_MODEL), lambda i: (i, 8)),
                  tile, _resident((D_MODEL, D_MODEL)), _resident((D_MODEL, D_MODEL)),
                  _resident((D_MODEL, D_MODEL))],
        out_specs=tile,
        compiler_params=pltpu.CompilerParams(
            dimension_semantics=("arbitrary",), vmem_limit_bytes=VMEM_LIMIT),
        name="merge",
    )(ya, yb, proj, proj, x1, wa, wb, wo)


def _token_tile(m):
    return 512 if m % 512 == 0 else m


def kernel(x_prompt, x_sample, state_sconv, state_gdn_conv, state_gdn, meta_tokens, norm_ffn1, ffn1_w_gate, ffn1_w_up, ffn1_w_down, norm_mix, w_in, sconv_w, gdn_conv_w, gdn_a_log, gdn_dt_bias, gdn_norm, w_a_out, w_b_out, w_o, norm_ffn2, ffn2_w_gate, ffn2_w_up, ffn2_w_down, norm_final):
    bsz, seq, _ = x_prompt.shape
    dec_b, dec_len, _ = x_sample.shape
    assert norm_ffn1.shape[0] == 1, "single layer"

    wg1, wu1, wd1 = (w[0].astype(BF16) for w in (ffn1_w_gate, ffn1_w_up, ffn1_w_down))
    wg2, wu2, wd2 = (w[0].astype(BF16) for w in (ffn2_w_gate, ffn2_w_up, ffn2_w_down))
    wa, wb, wo = (w[0].astype(BF16) for w in (w_a_out, w_b_out, w_o))
    n_main = 3 * D_MODEL + GDN_QKV
    n_ab = 2 * GDN_HEADS
    w_in0 = w_in[0]
    w_main = jnp.concatenate([w_in0[:, :n_main], w_in0[:, n_main + n_ab:]], axis=1).astype(BF16)
    w_ab = jnp.pad(w_in0[:, n_main:n_main + n_ab], ((0, 0), (0, LANES - n_ab))).astype(BF16)
    row = lambda v: v.reshape(1, -1).astype(F32)
    n1, nm, n2, nf = row(norm_ffn1[0]), row(norm_mix[0]), row(norm_ffn2[0]), row(norm_final)
    pad_heads = lambda v: jnp.pad(v.reshape(1, -1).astype(F32), ((0, 0), (0, LANES - GDN_HEADS)))
    alog, dtb = pad_heads(gdn_a_log[0]), pad_heads(gdn_dt_bias[0])
    gn = row(gdn_norm[0])
    scw, gcw = sconv_w[0].astype(F32), gdn_conv_w[0].astype(F32)
    mix_params = (scw, gcw, alog, dtb, gn)

    def front(x):
        tm = _token_tile(x.shape[0])
        x1, h = _ffn(x, n1, wg1, wu1, wd1, nm, final=False, tm=tm)
        proj, ab = _proj(h, w_main, w_ab, tm=tm)
        return x1, proj, ab

    def back(ya, yb, proj, x1):
        tm = _token_tile(x1.shape[0])
        x2 = _merge(ya, yb, proj, x1, wa, wb, wo, tm=tm)
        return _ffn(x2, n2, wg2, wu2, wd2, nf, final=True, tm=tm)

    _, proj_m, ab_m = front(meta_tokens.astype(F32))
    zeros = lambda *s: jnp.zeros(s, F32)
    _, _, sc_m, gc_m, s_m = _mixer(
        proj_m, ab_m, zeros(1, SC_TAPS - 1, D_MODEL), zeros(1, GDN_TAPS - 1, GDN_QKV),
        zeros(1, GDN_HEADS, GDN_DK, GDN_DV), *mix_params, G=1, C=N_META, NC=1)

    chunk = 64
    x1_p, proj_p, ab_p = front(x_prompt.reshape(bsz * seq, D_MODEL))
    ya_p, yb_p, p_sc, p_gc, p_gs = _mixer(
        proj_p, ab_p,
        jnp.broadcast_to(sc_m, (bsz,) + sc_m.shape[1:]),
        jnp.broadcast_to(gc_m, (bsz,) + gc_m.shape[1:]),
        jnp.broadcast_to(s_m, (bsz,) + s_m.shape[1:]),
        *mix_params, G=1, C=chunk, NC=seq // chunk)
    y_prompt = back(ya_p, yb_p, proj_p, x1_p).reshape(bsz, seq, D_MODEL)

    x1_s, proj_s, ab_s = front(x_sample.reshape(dec_b * dec_len, D_MODEL))
    ya_s, yb_s, s_sc, s_gc, s_gs = _mixer(
        proj_s, ab_s, state_sconv[0], state_gdn_conv[0], state_gdn[0],
        *mix_params, G=8, C=dec_len, NC=1)
    y_sample = back(ya_s, yb_s, proj_s, x1_s).reshape(dec_b, dec_len, D_MODEL)

    return (y_prompt, y_sample, p_sc[None], p_gc[None], p_gs[None],
            s_sc[None], s_gc[None], s_gs[None])
```

```python
import functools

import jax
import jax.numpy as jnp
from jax import lax
from jax.experimental import pallas as pl
from jax.experimental.pallas import tpu as pltpu

D_MODEL = 1024
D_FF = 2816
N_META = 16
SC_TAPS = 3
GDN_HEADS = 8
GDN_DK = 128
GDN_DV = 128
GDN_QKV = 3 * GDN_HEADS * GDN_DK
GDN_TAPS = 4
EPS = 1e-6
L2_EPS = 1e-6

LANES = 128
SUBLANES = 8
FF_CHUNK = 256
VMEM_LIMIT = 52 * 1024 * 1024

F32 = jnp.float32
BF16 = jnp.bfloat16


def _sigmoid(x):
    return 1.0 / (1.0 + jnp.exp(-x))


def _rms(x, g):
    return x * lax.rsqrt(jnp.mean(x * x, axis=-1, keepdims=True) + EPS) * g


def _dot(a, b):
    return jnp.dot(a, b, preferred_element_type=F32)


def _dot_nt(a, b):
    return lax.dot_general(a, b, (((1,), (1,)), ((), ())), preferred_element_type=F32)


def _dot_tn(a, b):
    return lax.dot_general(a, b, (((0,), (0,)), ((), ())), preferred_element_type=F32)


def _split2(x):
    hi = x.astype(BF16)
    lo = (x - hi.astype(F32)).astype(BF16)
    return hi, lo


def _split3(x):
    hi = x.astype(BF16)
    r = x - hi.astype(F32)
    mid = r.astype(BF16)
    lo = (r - mid.astype(F32)).astype(BF16)
    return hi, mid, lo


def _dot_hi(a, b):
    a_hi, a_lo = _split2(a)
    b_hi, b_lo = _split2(b)
    return _dot(a_hi, b_hi) + (_dot(a_hi, b_lo) + _dot(a_lo, b_hi))


def _dot_exact_lhs(m_bf16, x):
    hi, mid, lo = _split3(x)
    return _dot(m_bf16, hi) + (_dot(m_bf16, mid) + _dot(m_bf16, lo))


def _resident(shape):
    nd = len(shape)
    return pl.BlockSpec(shape, lambda *_: (0,) * nd, pipeline_mode=pl.Buffered(1))


def _ffn_kernel(x_ref, ng_ref, wg_ref, wu_ref, wd_ref, nn_ref, *rest, final):
    if final:
        y_ref, acc_ref = rest
    else:
        x1_ref, h_ref, acc_ref = rest
    x = x_ref[...]
    h = _rms(x, ng_ref[...]).astype(BF16)
    acc_ref[...] = jnp.zeros_like(acc_ref)

    def chunk(f, carry):
        off = pl.multiple_of(f * FF_CHUNK, FF_CHUNK)
        g = _dot(h, wg_ref[:, pl.ds(off, FF_CHUNK)])
        u = _dot(h, wu_ref[:, pl.ds(off, FF_CHUNK)])
        a = (g * _sigmoid(g) * u).astype(BF16)
        acc_ref[...] += _dot(a, wd_ref[pl.ds(off, FF_CHUNK), :])
        return carry

    lax.fori_loop(0, D_FF // FF_CHUNK, chunk, 0)
    x1 = x + 0.5 * acc_ref[...]
    if final:
        y_ref[...] = _rms(x1, nn_ref[...])
    else:
        x1_ref[...] = x1
        h_ref[...] = _rms(x1, nn_ref[...]).astype(BF16)


def _ffn(x, ng, wg, wu, wd, nn, *, final, tm):
    m = x.shape[0]
    tile = pl.BlockSpec((tm, D_MODEL), lambda i: (i, 0))
    if final:
        out_shape = jax.ShapeDtypeStruct((m, D_MODEL), F32)
        out_specs = tile
    else:
        out_shape = (jax.ShapeDtypeStruct((m, D_MODEL), F32),
                     jax.ShapeDtypeStruct((m, D_MODEL), BF16))
        out_specs = (tile, tile)
    return pl.pallas_call(
        functools.partial(_ffn_kernel, final=final),
        out_shape=out_shape,
        grid=(m // tm,),
        in_specs=[tile, _resident((1, D_MODEL)), _resident((D_MODEL, D_FF)),
                  _resident((D_MODEL, D_FF)), _resident((D_FF, D_MODEL)),
                  _resident((1, D_MODEL))],
        out_specs=out_specs,
        scratch_shapes=[pltpu.VMEM((tm, D_MODEL), F32)],
        compiler_params=pltpu.CompilerParams(
            dimension_semantics=("arbitrary",), vmem_limit_bytes=VMEM_LIMIT),
        name="ffn_final" if final else "ffn_mid",
    )(x, ng, wg, wu, wd, nn)


N_PROJ_BLOCKS = 9


def _proj_kernel(h_ref, w_ref, wab_ref, p_ref, ab_ref):
    j = pl.program_id(1)
    h = h_ref[...]
    off = pl.multiple_of(j * D_MODEL, D_MODEL)
    p_ref[...] = _dot(h, w_ref[:, pl.ds(off, D_MODEL)])

    @pl.when(j == 0)
    def _():
        ab_ref[...] = _dot(h, wab_ref[...])


def _proj(h, w_main, w_ab, *, tm):
    m = h.shape[0]
    return pl.pallas_call(
        _proj_kernel,
        out_shape=(jax.ShapeDtypeStruct((m, N_PROJ_BLOCKS * D_MODEL), F32),
                   jax.ShapeDtypeStruct((m, LANES), F32)),
        grid=(m // tm, N_PROJ_BLOCKS),
        in_specs=[pl.BlockSpec((tm, D_MODEL), lambda i, j: (i, 0)),
                  _resident((D_MODEL, N_PROJ_BLOCKS * D_MODEL)),
                  _resident((D_MODEL, LANES))],
        out_specs=(pl.BlockSpec((tm, D_MODEL), lambda i, j: (i, j)),
                   pl.BlockSpec((tm, LANES), lambda i, j: (i, 0))),
        compiler_params=pltpu.CompilerParams(
            dimension_semantics=("arbitrary", "arbitrary"), vmem_limit_bytes=VMEM_LIMIT),
        name="proj",
    )(h, w_main, w_ab)


HIST = SUBLANES


def _mixer_kernel(scb_ref, scc_ref, scx_ref, qkv_ref, z_ref, ab_ref,
                  sc0_ref, gc0_ref, s0_ref,
                  scw_ref, gcw_ref, alog_ref, dtb_ref, gn_ref,
                  ya_ref, yb_ref, sco_ref, gco_ref, so_ref,
                  ext_sc, ext_gc, *, G, C):
    R = G * C
    c = pl.program_id(1)

    @pl.when(c == 0)
    def _():
        ext_sc[:, HIST - (SC_TAPS - 1):HIST, :] = sc0_ref[...]
        ext_gc[:, HIST - (GDN_TAPS - 1):HIST, :] = gc0_ref[...]
        so_ref[...] = s0_ref[...]

    cx = scc_ref[...] * scx_ref[...]
    ext_sc[:, HIST:HIST + C, :] = cx.reshape(G, C, D_MODEL)
    conv = None
    for j in range(SC_TAPS):
        lo = HIST - (SC_TAPS - 1) + j
        term = ext_sc[:, lo:lo + C, :] * scw_ref[j:j + 1, :]
        conv = term if conv is None else conv + term
    ya_ref[...] = (scb_ref[...] * conv.reshape(R, D_MODEL)).astype(BF16)
    new_sc = ext_sc[:, HIST + C - (SC_TAPS - 1):HIST + C, :]
    ext_sc[:, HIST - (SC_TAPS - 1):HIST, :] = new_sc
    sco_ref[...] = new_sc

    ext_gc[:, HIST:HIST + C, :] = qkv_ref[...].reshape(G, C, GDN_QKV)
    conv = None
    for j in range(GDN_TAPS):
        lo = HIST - (GDN_TAPS - 1) + j
        term = ext_gc[:, lo:lo + C, :] * gcw_ref[j:j + 1, :]
        conv = term if conv is None else conv + term
    new_gc = ext_gc[:, HIST + C - (GDN_TAPS - 1):HIST + C, :]
    ext_gc[:, HIST - (GDN_TAPS - 1):HIST, :] = new_gc
    gco_ref[...] = new_gc
    conv = conv.reshape(R, GDN_QKV)
    qkv = conv * _sigmoid(conv)

    ab = ab_ref[...]
    xa = ab + dtb_ref[...]
    softplus = jnp.maximum(xa, 0.0) + jnp.log1p(jnp.exp(-jnp.abs(xa)))
    log_a = -jnp.exp(alog_ref[...]) * softplus
    beta_all = _sigmoid(ab)

    row = lax.broadcasted_iota(jnp.int32, (R, R), 0)
    col = lax.broadcasted_iota(jnp.int32, (R, R), 1)
    if G == 1:
        same = None
        incl = row >= col
        strict = row > col
        seg_all = jnp.ones((R, R), BF16)
    else:
        same = (row // C) == (col // C)
        incl = jnp.logical_and(same, row >= col)
        strict = jnp.logical_and(same, row > col)
        seg_all = jnp.where(same, 1.0, 0.0).astype(BF16)
    seg_tri = jnp.where(incl, 1.0, 0.0).astype(BF16)
    eye = jnp.where(row == col, 1.0, 0.0).astype(F32)

    g_col = _dot_exact_lhs(seg_tri, log_a)
    g_last = _dot_exact_lhs(seg_all, log_a)
    g_row = g_col.T
    exp_g = jnp.exp(g_col)
    exp_rest = jnp.exp(g_last - g_col)
    exp_last = jnp.exp(g_last)

    n_sq = max(C.bit_length() - 2, 0)

    heads = range(GDN_HEADS)
    n_key = GDN_HEADS * GDN_DK
    q, k, v, be, eg = [], [], [], [], []
    for h in heads:
        q_h = qkv[:, h * GDN_DK:(h + 1) * GDN_DK]
        k_h = qkv[:, n_key + h * GDN_DK:n_key + (h + 1) * GDN_DK]
        q.append(q_h * (lax.rsqrt(jnp.sum(q_h * q_h, axis=-1, keepdims=True) + L2_EPS)
                        * (GDN_DK ** -0.5)))
        k.append(k_h * lax.rsqrt(jnp.sum(k_h * k_h, axis=-1, keepdims=True) + L2_EPS))
        v.append(qkv[:, 2 * n_key + h * GDN_DV:2 * n_key + (h + 1) * GDN_DV])
        be.append(beta_all[:, GDN_HEADS + h:GDN_HEADS + h + 1])
        eg.append(exp_g[:, h:h + 1])

    a_qk, p, t = [], [], []
    for h in heads:
        dec = jnp.exp(jnp.minimum(g_col[:, h:h + 1] - g_row[h:h + 1, :], 0.0))
        k_b = k[h].astype(BF16)
        kk = _dot_nt(k_b, k_b)
        qk = _dot_nt(q[h].astype(BF16), k_b)
        p.append(jnp.where(strict, -(be[h] * kk * dec), 0.0))
        a_qk.append(jnp.where(incl, qk * dec, 0.0).astype(BF16))
        t.append(eye + p[h])

    for _ in range(n_sq):
        for h in heads:
            p_b = p[h].astype(BF16)
            t_b = t[h].astype(BF16)
            p[h] = _dot(p_b, p_b)
            t[h] = t[h] + _dot(t_b, p[h].astype(BF16))

    u, w = [], []
    for h in heads:
        rhs = jnp.concatenate([be[h] * v[h], (be[h] * eg[h]) * k[h]], axis=-1)
        sol = _dot(t[h].astype(BF16), rhs.astype(BF16))
        u.append(sol[:, :GDN_DV])
        w.append(sol[:, GDN_DV:])

    o = []
    for h in heads:
        q_dec = q[h] * eg[h]
        k_dec = (k[h] * exp_rest[:, h:h + 1]).astype(BF16)
        nus, qss = [], []
        for g in range(G):
            rs = slice(g * C, (g + 1) * C)
            s = so_ref[g, h]
            lhs = jnp.concatenate([w[h][rs], q_dec[rs]], axis=0).astype(BF16)
            r = _dot(lhs, s.astype(BF16))
            nus.append(u[h][rs] - r[:C])
            qss.append(r[C:])
        nu = nus[0] if G == 1 else jnp.concatenate(nus, axis=0)
        qs = qss[0] if G == 1 else jnp.concatenate(qss, axis=0)
        nu_b = nu.astype(BF16)
        o.append(qs + _dot(a_qk[h], nu_b))
        for g in range(G):
            rs = slice(g * C, (g + 1) * C)
            el = exp_last[g * C:g * C + 1, h:h + 1]
            so_ref[g, h] = el * so_ref[g, h] + _dot_tn(k_dec[rs], nu_b[rs])

    for h in heads:
        hs = slice(h * GDN_DV, (h + 1) * GDN_DV)
        on = o[h] * lax.rsqrt(jnp.mean(o[h] * o[h], axis=-1, keepdims=True) + EPS) * gn_ref[...]
        z_h = z_ref[:, hs]
        yb_ref[:, hs] = (on * (z_h * _sigmoid(z_h))).astype(BF16)


def _mixer(proj, ab, sc0, gc0, s0, scw, gcw, alog, dtb, gn, *, G, C, NC):
    R = G * C
    m = proj.shape[0]
    nseq = sc0.shape[0]
    nb = nseq // G
    assert m == nb * NC * R
    rows = lambda i, c: i * NC + c

    def col(blk, width=D_MODEL):
        return pl.BlockSpec((R, width), lambda i, c: (rows(i, c), blk))

    seq3 = lambda shape: pl.BlockSpec((G,) + shape, lambda i, c: (i, 0, 0))
    state = pl.BlockSpec((G, GDN_HEADS, GDN_DK, GDN_DV), lambda i, c: (i, 0, 0, 0))
    return pl.pallas_call(
        functools.partial(_mixer_kernel, G=G, C=C),
        out_shape=(jax.ShapeDtypeStruct((m, D_MODEL), BF16),
                   jax.ShapeDtypeStruct((m, D_MODEL), BF16),
                   jax.ShapeDtypeStruct((nseq, SC_TAPS - 1, D_MODEL), F32),
                   jax.ShapeDtypeStruct((nseq, GDN_TAPS - 1, GDN_QKV), F32),
                   jax.ShapeDtypeStruct((nseq, GDN_HEADS, GDN_DK, GDN_DV), F32)),
        grid=(nb, NC),
        in_specs=[col(0), col(1), col(2), col(1, GDN_QKV), col(6),
                  pl.BlockSpec((R, LANES), lambda i, c: (rows(i, c), 0)),
                  seq3((SC_TAPS - 1, D_MODEL)), seq3((GDN_TAPS - 1, GDN_QKV)), state,
                  _resident((SC_TAPS, D_MODEL)), _resident((GDN_TAPS, GDN_QKV)),
                  _resident((1, LANES)), _resident((1, LANES)), _resident((1, GDN_DV))],
        out_specs=(col(0), col(0),
                   seq3((SC_TAPS - 1, D_MODEL)), seq3((GDN_TAPS - 1, GDN_QKV)), state),
        scratch_shapes=[pltpu.VMEM((G, HIST + C, D_MODEL), F32),
                        pltpu.VMEM((G, HIST + C, GDN_QKV), F32)],
        compiler_params=pltpu.CompilerParams(
            dimension_semantics=("arbitrary", "arbitrary"), vmem_limit_bytes=VMEM_LIMIT),
        name=f"mixer_g{G}_c{C}",
    )(proj, proj, proj, proj, proj, ab, sc0, gc0, s0, scw, gcw, alog, dtb, gn)


def _merge_kernel(ya_ref, yb_ref, ga_ref, gb_ref, x1_ref, wa_ref, wb_ref, wo_ref, x2_ref):
    merged = (_sigmoid(ga_ref[...]) * _dot(ya_ref[...], wa_ref[...])
              + _sigmoid(gb_ref[...]) * _dot(yb_ref[...], wb_ref[...]))
    x2_ref[...] = x1_ref[...] + _dot(merged.astype(BF16), wo_ref[...])


def _merge(ya, yb, proj, x1, wa, wb, wo, *, tm):
    m = x1.shape[0]
    tile = pl.BlockSpec((tm, D_MODEL), lambda i: (i, 0))
    return pl.pallas_call(
        _merge_kernel,
        out_shape=jax.ShapeDtypeStruct((m, D_MODEL), F32),
        grid=(m // tm,),
        in_specs=[tile, tile,
                  pl.BlockSpec((tm, D_MODEL), lambda i: (i, 7)),
                  pl.BlockSpec((tm, D_MODEL), lambda i: (i, 8)),
                  tile, _resident((D_MODEL, D_MODEL)), _resident((D_MODEL, D_MODEL)),
                  _resident((D_MODEL, D_MODEL))],
        out_specs=tile,
        compiler_params=pltpu.CompilerParams(
            dimension_semantics=("arbitrary",), vmem_limit_bytes=VMEM_LIMIT),
        name="merge",
    )(ya, yb, proj, proj, x1, wa, wb, wo)


def _token_tile(m):
    return 512 if m % 512 == 0 else m


def kernel(x_prompt, x_sample, state_sconv, state_gdn_conv, state_gdn, meta_tokens, norm_ffn1, ffn1_w_gate, ffn1_w_up, ffn1_w_down, norm_mix, w_in, sconv_w, gdn_conv_w, gdn_a_log, gdn_dt_bias, gdn_norm, w_a_out, w_b_out, w_o, norm_ffn2, ffn2_w_gate, ffn2_w_up, ffn2_w_down, norm_final):
    bsz, seq, _ = x_prompt.shape
    dec_b, dec_len, _ = x_sample.shape
    assert norm_ffn1.shape[0] == 1, "single layer"

    wg1, wu1, wd1 = (w[0].astype(BF16) for w in (ffn1_w_gate, ffn1_w_up, ffn1_w_down))
    wg2, wu2, wd2 = (w[0].astype(BF16) for w in (ffn2_w_gate, ffn2_w_up, ffn2_w_down))
    wa, wb, wo = (w[0].astype(BF16) for w in (w_a_out, w_b_out, w_o))
    n_main = 3 * D_MODEL + GDN_QKV
    n_ab = 2 * GDN_HEADS
    w_in0 = w_in[0]
    w_main = jnp.concatenate([w_in0[:, :n_main], w_in0[:, n_main + n_ab:]], axis=1).astype(BF16)
    w_ab = jnp.pad(w_in0[:, n_main:n_main + n_ab], ((0, 0), (0, LANES - n_ab))).astype(BF16)
    row = lambda v: v.reshape(1, -1).astype(F32)
    n1, nm, n2, nf = row(norm_ffn1[0]), row(norm_mix[0]), row(norm_ffn2[0]), row(norm_final)
    pad_heads = lambda v: jnp.pad(v.reshape(1, -1).astype(F32), ((0, 0), (0, LANES - GDN_HEADS)))
    alog, dtb = pad_heads(gdn_a_log[0]), pad_heads(gdn_dt_bias[0])
    gn = row(gdn_norm[0])
    scw, gcw = sconv_w[0].astype(F32), gdn_conv_w[0].astype(F32)
    mix_params = (scw, gcw, alog, dtb, gn)

    def front(x):
        tm = _token_tile(x.shape[0])
        x1, h = _ffn(x, n1, wg1, wu1, wd1, nm, final=False, tm=tm)
        proj, ab = _proj(h, w_main, w_ab, tm=tm)
        return x1, proj, ab

    def back(ya, yb, proj, x1):
        tm = _token_tile(x1.shape[0])
        x2 = _merge(ya, yb, proj, x1, wa, wb, wo, tm=tm)
        return _ffn(x2, n2, wg2, wu2, wd2, nf, final=True, tm=tm)

    _, proj_m, ab_m = front(meta_tokens.astype(F32))
    zeros = lambda *s: jnp.zeros(s, F32)
    _, _, sc_m, gc_m, s_m = _mixer(
        proj_m, ab_m, zeros(1, SC_TAPS - 1, D_MODEL), zeros(1, GDN_TAPS - 1, GDN_QKV),
        zeros(1, GDN_HEADS, GDN_DK, GDN_DV), *mix_params, G=1, C=N_META, NC=1)

    chunk = 64
    x1_p, proj_p, ab_p = front(x_prompt.reshape(bsz * seq, D_MODEL))
    ya_p, yb_p, p_sc, p_gc, p_gs = _mixer(
        proj_p, ab_p,
        jnp.broadcast_to(sc_m, (bsz,) + sc_m.shape[1:]),
        jnp.broadcast_to(gc_m, (bsz,) + gc_m.shape[1:]),
        jnp.broadcast_to(s_m, (bsz,) + s_m.shape[1:]),
        *mix_params, G=1, C=chunk, NC=seq // chunk)
    y_prompt = back(ya_p, yb_p, proj_p, x1_p).reshape(bsz, seq, D_MODEL)

    x1_s, proj_s, ab_s = front(x_sample.reshape(dec_b * dec_len, D_MODEL))
    ya_s, yb_s, s_sc, s_gc, s_gs = _mixer(
        proj_s, ab_s, state_sconv[0], state_gdn_conv[0], state_gdn[0],
        *mix_params, G=8, C=dec_len, NC=1)
    y_sample = back(ya_s, yb_s, proj_s, x1_s).reshape(dec_b, dec_len, D_MODEL)

    return (y_prompt, y_sample, p_sc[None], p_gc[None], p_gs[None],
            s_sc[None], s_gc[None], s_gs[None])
```

```python
import functools

import jax
import jax.numpy as jnp
from jax import lax
from jax.experimental import pallas as pl
from jax.experimental.pallas import tpu as pltpu

D_MODEL = 1024
D_FF = 2816
N_META = 16
SC_TAPS = 3
GDN_HEADS = 8
GDN_DK = 128
GDN_DV = 128
GDN_QKV = 3 * GDN_HEADS * GDN_DK
GDN_TAPS = 4
EPS = 1e-6
L2_EPS = 1e-6

LANES = 128
SUBLANES = 8
FF_CHUNK = 256
VMEM_LIMIT = 52 * 1024 * 1024

F32 = jnp.float32
BF16 = jnp.bfloat16


def _sigmoid(x):
    return 1.0 / (1.0 + jnp.exp(-x))


def _rms(x, g):
    return x * lax.rsqrt(jnp.mean(x * x, axis=-1, keepdims=True) + EPS) * g


def _dot(a, b):
    return jnp.dot(a, b, preferred_element_type=F32)


def _dot_nt(a, b):
    return lax.dot_general(a, b, (((1,), (1,)), ((), ())), preferred_element_type=F32)


def _dot_tn(a, b):
    return lax.dot_general(a, b, (((0,), (0,)), ((), ())), preferred_element_type=F32)


def _split2(x):
    hi = x.astype(BF16)
    lo = (x - hi.astype(F32)).astype(BF16)
    return hi, lo


def _split3(x):
    hi = x.astype(BF16)
    r = x - hi.astype(F32)
    mid = r.astype(BF16)
    lo = (r - mid.astype(F32)).astype(BF16)
    return hi, mid, lo


def _dot_hi(a, b):
    a_hi, a_lo = _split2(a)
    b_hi, b_lo = _split2(b)
    return _dot(a_hi, b_hi) + (_dot(a_hi, b_lo) + _dot(a_lo, b_hi))


def _dot_exact_lhs(m_bf16, x):
    hi, mid, lo = _split3(x)
    return _dot(m_bf16, hi) + (_dot(m_bf16, mid) + _dot(m_bf16, lo))


def _resident(shape):
    nd = len(shape)
    return pl.BlockSpec(shape, lambda *_: (0,) * nd, pipeline_mode=pl.Buffered(1))


def _ffn_kernel(x_ref, ng_ref, wg_ref, wu_ref, wd_ref, nn_ref, *rest, final):
    if final:
        (y_ref,) = rest
    else:
        x1_ref, h_ref = rest
    x = x_ref[...]
    h = _rms(x, ng_ref[...]).astype(BF16)
    acc = None
    for f in range(D_FF // FF_CHUNK):
        cols = slice(f * FF_CHUNK, (f + 1) * FF_CHUNK)
        g = _dot(h, wg_ref[:, cols])
        u = _dot(h, wu_ref[:, cols])
        a = (g * _sigmoid(g) * u).astype(BF16)
        d = _dot(a, wd_ref[cols, :])
        acc = d if acc is None else acc + d
    x1 = x + 0.5 * acc
    if final:
        y_ref[...] = _rms(x1, nn_ref[...])
    else:
        x1_ref[...] = x1
        h_ref[...] = _rms(x1, nn_ref[...]).astype(BF16)


def _ffn(x, ng, wg, wu, wd, nn, *, final, tm):
    m = x.shape[0]
    tile = pl.BlockSpec((tm, D_MODEL), lambda i: (i, 0))
    if final:
        out_shape = jax.ShapeDtypeStruct((m, D_MODEL), F32)
        out_specs = tile
    else:
        out_shape = (jax.ShapeDtypeStruct((m, D_MODEL), F32),
                     jax.ShapeDtypeStruct((m, D_MODEL), BF16))
        out_specs = (tile, tile)
    return pl.pallas_call(
        functools.partial(_ffn_kernel, final=final),
        out_shape=out_shape,
        grid=(m // tm,),
        in_specs=[tile, _resident((1, D_MODEL)), _resident((D_MODEL, D_FF)),
                  _resident((D_MODEL, D_FF)), _resident((D_FF, D_MODEL)),
                  _resident((1, D_MODEL))],
        out_specs=out_specs,
        compiler_params=pltpu.CompilerParams(
            dimension_semantics=("arbitrary",), vmem_limit_bytes=VMEM_LIMIT),
        name="ffn_final" if final else "ffn_mid",
    )(x, ng, wg, wu, wd, nn)


N_PROJ_BLOCKS = 9


def _proj_kernel(h_ref, w_ref, wab_ref, p_ref, ab_ref):
    j = pl.program_id(1)
    h = h_ref[...]
    off = pl.multiple_of(j * D_MODEL, D_MODEL)
    p_ref[...] = _dot(h, w_ref[:, pl.ds(off, D_MODEL)])

    @pl.when(j == 0)
    def _():
        ab_ref[...] = _dot(h, wab_ref[...])


def _proj(h, w_main, w_ab, *, tm):
    m = h.shape[0]
    return pl.pallas_call(
        _proj_kernel,
        out_shape=(jax.ShapeDtypeStruct((m, N_PROJ_BLOCKS * D_MODEL), F32),
                   jax.ShapeDtypeStruct((m, LANES), F32)),
        grid=(m // tm, N_PROJ_BLOCKS),
        in_specs=[pl.BlockSpec((tm, D_MODEL), lambda i, j: (i, 0)),
                  _resident((D_MODEL, N_PROJ_BLOCKS * D_MODEL)),
                  _resident((D_MODEL, LANES))],
        out_specs=(pl.BlockSpec((tm, D_MODEL), lambda i, j: (i, j)),
                   pl.BlockSpec((tm, LANES), lambda i, j: (i, 0))),
        compiler_params=pltpu.CompilerParams(
            dimension_semantics=("arbitrary", "arbitrary"), vmem_limit_bytes=VMEM_LIMIT),
        name="proj",
    )(h, w_main, w_ab)


HIST = SUBLANES


def _mixer_kernel(scb_ref, scc_ref, scx_ref, qkv_ref, z_ref, ab_ref,
                  sc0_ref, gc0_ref, s0_ref,
                  scw_ref, gcw_ref, alog_ref, dtb_ref, gn_ref,
                  ya_ref, yb_ref, sco_ref, gco_ref, so_ref,
                  ext_sc, ext_gc, *, G, C):
    R = G * C
    c = pl.program_id(1)

    @pl.when(c == 0)
    def _():
        ext_sc[:, HIST - (SC_TAPS - 1):HIST, :] = sc0_ref[...]
        ext_gc[:, HIST - (GDN_TAPS - 1):HIST, :] = gc0_ref[...]
        so_ref[...] = s0_ref[...]

    ext_sc[:, HIST:HIST + C, :] = scc_ref[...] * scx_ref[...]
    conv = None
    for j in range(SC_TAPS):
        lo = HIST - (SC_TAPS - 1) + j
        term = ext_sc[:, lo:lo + C, :] * scw_ref[j:j + 1, :]
        conv = term if conv is None else conv + term
    y_a = (scb_ref[...] * conv).reshape(R, D_MODEL).astype(BF16)
    ya_ref[...] = y_a.reshape(ya_ref.shape)
    new_sc = ext_sc[:, HIST + C - (SC_TAPS - 1):HIST + C, :]
    ext_sc[:, HIST - (SC_TAPS - 1):HIST, :] = new_sc
    sco_ref[...] = new_sc

    ext_gc[:, HIST:HIST + C, :] = qkv_ref[...]
    conv = None
    for j in range(GDN_TAPS):
        lo = HIST - (GDN_TAPS - 1) + j
        term = ext_gc[:, lo:lo + C, :] * gcw_ref[j:j + 1, :]
        conv = term if conv is None else conv + term
    new_gc = ext_gc[:, HIST + C - (GDN_TAPS - 1):HIST + C, :]
    ext_gc[:, HIST - (GDN_TAPS - 1):HIST, :] = new_gc
    gco_ref[...] = new_gc
    conv = conv.reshape(R, GDN_QKV)
    qkv = conv * _sigmoid(conv)

    ab = ab_ref[...].reshape(R, LANES)
    xa = ab + dtb_ref[...]
    softplus = jnp.maximum(xa, 0.0) + jnp.log1p(jnp.exp(-jnp.abs(xa)))
    log_a = -jnp.exp(alog_ref[...]) * softplus
    beta_all = _sigmoid(ab)

    row = lax.broadcasted_iota(jnp.int32, (R, R), 0)
    col = lax.broadcasted_iota(jnp.int32, (R, R), 1)
    if G == 1:
        same = None
        incl = row >= col
        strict = row > col
        seg_all = jnp.ones((R, R), BF16)
    else:
        same = (row // C) == (col // C)
        incl = jnp.logical_and(same, row >= col)
        strict = jnp.logical_and(same, row > col)
        seg_all = jnp.where(same, 1.0, 0.0).astype(BF16)
    seg_tri = jnp.where(incl, 1.0, 0.0).astype(BF16)
    eye = jnp.where(row == col, 1.0, 0.0).astype(F32)

    g_col = _dot_exact_lhs(seg_tri, log_a)
    g_last = _dot_exact_lhs(seg_all, log_a)
    g_row = g_col.T
    exp_g = jnp.exp(g_col)
    exp_rest = jnp.exp(g_last - g_col)
    exp_last = jnp.exp(g_last)

    n_sq = max(C.bit_length() - 2, 0)

    heads = range(GDN_HEADS)
    n_key = GDN_HEADS * GDN_DK
    q, k, v, be, eg = [], [], [], [], []
    for h in heads:
        q_h = qkv[:, h * GDN_DK:(h + 1) * GDN_DK]
        k_h = qkv[:, n_key + h * GDN_DK:n_key + (h + 1) * GDN_DK]
        q.append(q_h * (lax.rsqrt(jnp.sum(q_h * q_h, axis=-1, keepdims=True) + L2_EPS)
                        * (GDN_DK ** -0.5)))
        k.append(k_h * lax.rsqrt(jnp.sum(k_h * k_h, axis=-1, keepdims=True) + L2_EPS))
        v.append(qkv[:, 2 * n_key + h * GDN_DV:2 * n_key + (h + 1) * GDN_DV])
        be.append(beta_all[:, GDN_HEADS + h:GDN_HEADS + h + 1])
        eg.append(exp_g[:, h:h + 1])

    a_qk, p, t = [], [], []
    for h in heads:
        dec = jnp.exp(jnp.minimum(g_col[:, h:h + 1] - g_row[h:h + 1, :], 0.0))
        k_b = k[h].astype(BF16)
        kk = _dot_nt(k_b, k_b)
        qk = _dot_nt(q[h].astype(BF16), k_b)
        p.append(jnp.where(strict, -(be[h] * kk * dec), 0.0))
        a_qk.append(jnp.where(incl, qk * dec, 0.0).astype(BF16))
        t.append(eye + p[h])

    for _ in range(n_sq):
        for h in heads:
            p_b = p[h].astype(BF16)
            t_b = t[h].astype(BF16)
            p[h] = _dot(p_b, p_b)
            t[h] = t[h] + _dot(t_b, p[h].astype(BF16))

    u, w = [], []
    for h in heads:
        rhs = jnp.concatenate([be[h] * v[h], (be[h] * eg[h]) * k[h]], axis=-1)
        sol = _dot(t[h].astype(BF16), rhs.astype(BF16))
        u.append(sol[:, :GDN_DV])
        w.append(sol[:, GDN_DV:])

    o = []
    for h in heads:
        q_dec = q[h] * eg[h]
        k_dec = (k[h] * exp_rest[:, h:h + 1]).astype(BF16)
        nus, qss = [], []
        for g in range(G):
            rs = slice(g * C, (g + 1) * C)
            s = so_ref[g, h]
            lhs = jnp.concatenate([w[h][rs], q_dec[rs]], axis=0).astype(BF16)
            r = _dot(lhs, s.astype(BF16))
            nus.append(u[h][rs] - r[:C])
            qss.append(r[C:])
        nu = nus[0] if G == 1 else jnp.concatenate(nus, axis=0)
        qs = qss[0] if G == 1 else jnp.concatenate(qss, axis=0)
        nu_b = nu.astype(BF16)
        o.append(qs + _dot(a_qk[h], nu_b))
        for g in range(G):
            rs = slice(g * C, (g + 1) * C)
            el = exp_last[g * C:g * C + 1, h:h + 1]
            so_ref[g, h] = el * so_ref[g, h] + _dot_tn(k_dec[rs], nu_b[rs])

    for h in heads:
        hs = slice(h * GDN_DV, (h + 1) * GDN_DV)
        on = o[h] * lax.rsqrt(jnp.mean(o[h] * o[h], axis=-1, keepdims=True) + EPS) * gn_ref[...]
        z_h = z_ref[:, :, hs].reshape(R, GDN_DV)
        y_h = (on * (z_h * _sigmoid(z_h))).astype(BF16)
        if len(yb_ref.shape) == 3:
            yb_ref[:, :, hs] = y_h.reshape(G, C, GDN_DV)
        else:
            yb_ref[:, hs] = y_h


def _mixer(proj, ab, sc0, gc0, s0, scw, gcw, alog, dtb, gn, *, G, C, NC):
    R = G * C
    m = proj.shape[0]
    nseq = sc0.shape[0]
    nb = nseq // G
    seq_len = NC * C
    assert m == nseq * seq_len
    proj3 = proj.reshape(nseq, seq_len, proj.shape[1])
    ab3 = ab.reshape(nseq, seq_len, LANES)

    def col(blk, width=D_MODEL):
        return pl.BlockSpec((G, C, width), lambda i, c: (i, c, blk))

    if C % (2 * SUBLANES) == 0:
        y_shape = jax.ShapeDtypeStruct((nseq, seq_len, D_MODEL), BF16)
        y_spec = col(0)
    else:
        assert NC == 1
        y_shape = jax.ShapeDtypeStruct((m, D_MODEL), BF16)
        y_spec = pl.BlockSpec((R, D_MODEL), lambda i, c: (i, 0))
    seq3 = lambda shape: pl.BlockSpec((G,) + shape, lambda i, c: (i, 0, 0))
    state = pl.BlockSpec((G, GDN_HEADS, GDN_DK, GDN_DV), lambda i, c: (i, 0, 0, 0))
    ya, yb, sc, gc, gs = pl.pallas_call(
        functools.partial(_mixer_kernel, G=G, C=C),
        out_shape=(y_shape, y_shape,
                   jax.ShapeDtypeStruct((nseq, SC_TAPS - 1, D_MODEL), F32),
                   jax.ShapeDtypeStruct((nseq, GDN_TAPS - 1, GDN_QKV), F32),
                   jax.ShapeDtypeStruct((nseq, GDN_HEADS, GDN_DK, GDN_DV), F32)),
        grid=(nb, NC),
        in_specs=[col(0), col(1), col(2), col(1, GDN_QKV), col(6),
                  pl.BlockSpec((G, C, LANES), lambda i, c: (i, c, 0)),
                  seq3((SC_TAPS - 1, D_MODEL)), seq3((GDN_TAPS - 1, GDN_QKV)), state,
                  _resident((SC_TAPS, D_MODEL)), _resident((GDN_TAPS, GDN_QKV)),
                  _resident((1, LANES)), _resident((1, LANES)), _resident((1, GDN_DV))],
        out_specs=(y_spec, y_spec,
                   seq3((SC_TAPS - 1, D_MODEL)), seq3((GDN_TAPS - 1, GDN_QKV)), state),
        scratch_shapes=[pltpu.VMEM((G, HIST + C, D_MODEL), F32),
                        pltpu.VMEM((G, HIST + C, GDN_QKV), F32)],
        compiler_params=pltpu.CompilerParams(
            dimension_semantics=("arbitrary", "arbitrary"), vmem_limit_bytes=VMEM_LIMIT),
        name=f"mixer_g{G}_c{C}",
    )(proj3, proj3, proj3, proj3, proj3, ab3, sc0, gc0, s0, scw, gcw, alog, dtb, gn)
    return ya.reshape(m, D_MODEL), yb.reshape(m, D_MODEL), sc, gc, gs


def _merge_kernel(ya_ref, yb_ref, ga_ref, gb_ref, x1_ref, wa_ref, wb_ref, wo_ref, x2_ref):
    merged = (_sigmoid(ga_ref[...]) * _dot(ya_ref[...], wa_ref[...])
              + _sigmoid(gb_ref[...]) * _dot(yb_ref[...], wb_ref[...]))
    x2_ref[...] = x1_ref[...] + _dot(merged.astype(BF16), wo_ref[...])


def _merge(ya, yb, proj, x1, wa, wb, wo, *, tm):
    m = x1.shape[0]
    tile = pl.BlockSpec((tm, D_MODEL), lambda i: (i, 0))
    return pl.pallas_call(
        _merge_kernel,
        out_shape=jax.ShapeDtypeStruct((m, D_MODEL), F32),
        grid=(m // tm,),
        in_specs=[tile, tile,
                  pl.BlockSpec((tm, D_MODEL), lambda i: (i, 7)),
                  pl.BlockSpec((tm, D_MODEL), lambda i: (i, 8)),
                  tile, _resident((D_MODEL, D_MODEL)), _resident((D_MODEL, D_MODEL)),
                  _resident((D_MODEL, D_MODEL))],
        out_specs=tile,
        compiler_params=pltpu.CompilerParams(
            dimension_semantics=("arbitrary",), vmem_limit_bytes=VMEM_LIMIT),
        name="merge",
    )(ya, yb, proj, proj, x1, wa, wb, wo)


def _token_tile(m, tm=512):
    return tm if m % tm == 0 else m


def kernel(x_prompt, x_sample, state_sconv, state_gdn_conv, state_gdn, meta_tokens, norm_ffn1, ffn1_w_gate, ffn1_w_up, ffn1_w_down, norm_mix, w_in, sconv_w, gdn_conv_w, gdn_a_log, gdn_dt_bias, gdn_norm, w_a_out, w_b_out, w_o, norm_ffn2, ffn2_w_gate, ffn2_w_up, ffn2_w_down, norm_final):
    bsz, seq, _ = x_prompt.shape
    dec_b, dec_len, _ = x_sample.shape
    assert norm_ffn1.shape[0] == 1, "single layer"

    wg1, wu1, wd1 = (w[0].astype(BF16) for w in (ffn1_w_gate, ffn1_w_up, ffn1_w_down))
    wg2, wu2, wd2 = (w[0].astype(BF16) for w in (ffn2_w_gate, ffn2_w_up, ffn2_w_down))
    wa, wb, wo = (w[0].astype(BF16) for w in (w_a_out, w_b_out, w_o))
    n_main = 3 * D_MODEL + GDN_QKV
    n_ab = 2 * GDN_HEADS
    w_in0 = w_in[0]
    w_main = jnp.concatenate([w_in0[:, :n_main], w_in0[:, n_main + n_ab:]], axis=1).astype(BF16)
    w_ab = jnp.pad(w_in0[:, n_main:n_main + n_ab], ((0, 0), (0, LANES - n_ab))).astype(BF16)
    row = lambda v: v.reshape(1, -1).astype(F32)
    n1, nm, n2, nf = row(norm_ffn1[0]), row(norm_mix[0]), row(norm_ffn2[0]), row(norm_final)
    pad_heads = lambda v: jnp.pad(v.reshape(1, -1).astype(F32), ((0, 0), (0, LANES - GDN_HEADS)))
    alog, dtb = pad_heads(gdn_a_log[0]), pad_heads(gdn_dt_bias[0])
    gn = row(gdn_norm[0])
    scw, gcw = sconv_w[0].astype(F32), gdn_conv_w[0].astype(F32)
    mix_params = (scw, gcw, alog, dtb, gn)

    def front(x):
        tm = _token_tile(x.shape[0])
        x1, h = _ffn(x, n1, wg1, wu1, wd1, nm, final=False, tm=tm)
        proj, ab = _proj(h, w_main, w_ab, tm=_token_tile(x.shape[0], 1024))
        return x1, proj, ab

    def back(ya, yb, proj, x1):
        tm = _token_tile(x1.shape[0])
        x2 = _merge(ya, yb, proj, x1, wa, wb, wo, tm=tm)
        return _ffn(x2, n2, wg2, wu2, wd2, nf, final=True, tm=tm)

    _, proj_m, ab_m = front(meta_tokens.astype(F32))
    zeros = lambda *s: jnp.zeros(s, F32)
    _, _, sc_m, gc_m, s_m = _mixer(
        proj_m, ab_m, zeros(1, SC_TAPS - 1, D_MODEL), zeros(1, GDN_TAPS - 1, GDN_QKV),
        zeros(1, GDN_HEADS, GDN_DK, GDN_DV), *mix_params, G=1, C=N_META, NC=1)

    chunk = 64
    x1_p, proj_p, ab_p = front(x_prompt.reshape(bsz * seq, D_MODEL))
    ya_p, yb_p, p_sc, p_gc, p_gs = _mixer(
        proj_p, ab_p,
        jnp.broadcast_to(sc_m, (bsz,) + sc_m.shape[1:]),
        jnp.broadcast_to(gc_m, (bsz,) + gc_m.shape[1:]),
        jnp.broadcast_to(s_m, (bsz,) + s_m.shape[1:]),
        *mix_params, G=2, C=chunk, NC=seq // chunk)
    y_prompt = back(ya_p, yb_p, proj_p, x1_p).reshape(bsz, seq, D_MODEL)

    x1_s, proj_s, ab_s = front(x_sample.reshape(dec_b * dec_len, D_MODEL))
    ya_s, yb_s, s_sc, s_gc, s_gs = _mixer(
        proj_s, ab_s, state_sconv[0], state_gdn_conv[0], state_gdn[0],
        *mix_params, G=8, C=dec_len, NC=1)
    y_sample = back(ya_s, yb_s, proj_s, x1_s).reshape(dec_b, dec_len, D_MODEL)

    return (y_prompt, y_sample, p_sc[None], p_gc[None], p_gs[None],
            s_sc[None], s_gc[None], s_gs[None])
```

```python
import functools

import jax
import jax.numpy as jnp
from jax import lax
from jax.experimental import pallas as pl
from jax.experimental.pallas import tpu as pltpu

D_MODEL = 1024
D_FF = 2816
N_META = 16
SC_TAPS = 3
GDN_HEADS = 8
GDN_DK = 128
GDN_DV = 128
GDN_QKV = 3 * GDN_HEADS * GDN_DK
GDN_TAPS = 4
EPS = 1e-6
L2_EPS = 1e-6

LANES = 128
SUBLANES = 8
FF_CHUNK = 256
VMEM_LIMIT = 52 * 1024 * 1024

F32 = jnp.float32
BF16 = jnp.bfloat16


def _sigmoid(x):
    return 1.0 / (1.0 + jnp.exp(-x))


def _rms(x, g):
    return x * lax.rsqrt(jnp.mean(x * x, axis=-1, keepdims=True) + EPS) * g


def _dot(a, b):
    return jnp.dot(a, b, preferred_element_type=F32)


def _dot_nt(a, b):
    return lax.dot_general(a, b, (((1,), (1,)), ((), ())), preferred_element_type=F32)


def _dot_tn(a, b):
    return lax.dot_general(a, b, (((0,), (0,)), ((), ())), preferred_element_type=F32)


def _split2(x):
    hi = x.astype(BF16)
    lo = (x - hi.astype(F32)).astype(BF16)
    return hi, lo


def _split3(x):
    hi = x.astype(BF16)
    r = x - hi.astype(F32)
    mid = r.astype(BF16)
    lo = (r - mid.astype(F32)).astype(BF16)
    return hi, mid, lo


def _dot_hi(a, b):
    a_hi, a_lo = _split2(a)
    b_hi, b_lo = _split2(b)
    return _dot(a_hi, b_hi) + (_dot(a_hi, b_lo) + _dot(a_lo, b_hi))


def _dot_exact_lhs(m_bf16, x):
    hi, mid, lo = _split3(x)
    return _dot(m_bf16, hi) + (_dot(m_bf16, mid) + _dot(m_bf16, lo))


def _resident(shape):
    nd = len(shape)
    return pl.BlockSpec(shape, lambda *_: (0,) * nd, pipeline_mode=pl.Buffered(1))


def _ffn_kernel(x_ref, ng_ref, wg_ref, wu_ref, wd_ref, nn_ref, *rest, final):
    if final:
        (y_ref,) = rest
    else:
        x1_ref, h_ref = rest
    x = x_ref[...]
    h = _rms(x, ng_ref[...]).astype(BF16)
    acc = None
    for f in range(D_FF // FF_CHUNK):
        cols = slice(f * FF_CHUNK, (f + 1) * FF_CHUNK)
        g = _dot(h, wg_ref[:, cols])
        u = _dot(h, wu_ref[:, cols])
        a = (g * _sigmoid(g) * u).astype(BF16)
        d = _dot(a, wd_ref[cols, :])
        acc = d if acc is None else acc + d
    x1 = x + 0.5 * acc
    if final:
        y_ref[...] = _rms(x1, nn_ref[...])
    else:
        x1_ref[...] = x1
        h_ref[...] = _rms(x1, nn_ref[...]).astype(BF16)


def _ffn(x, ng, wg, wu, wd, nn, *, final, tm):
    m = x.shape[0]
    tile = pl.BlockSpec((tm, D_MODEL), lambda i: (i, 0))
    if final:
        out_shape = jax.ShapeDtypeStruct((m, D_MODEL), F32)
        out_specs = tile
    else:
        out_shape = (jax.ShapeDtypeStruct((m, D_MODEL), F32),
                     jax.ShapeDtypeStruct((m, D_MODEL), BF16))
        out_specs = (tile, tile)
    return pl.pallas_call(
        functools.partial(_ffn_kernel, final=final),
        out_shape=out_shape,
        grid=(m // tm,),
        in_specs=[tile, _resident((1, D_MODEL)), _resident((D_MODEL, D_FF)),
                  _resident((D_MODEL, D_FF)), _resident((D_FF, D_MODEL)),
                  _resident((1, D_MODEL))],
        out_specs=out_specs,
        compiler_params=pltpu.CompilerParams(
            dimension_semantics=("arbitrary",), vmem_limit_bytes=VMEM_LIMIT),
        name="ffn_final" if final else "ffn_mid",
    )(x, ng, wg, wu, wd, nn)


N_PROJ_BLOCKS = 9


def _proj_kernel(h_ref, w_ref, wab_ref, p_ref, ab_ref):
    j = pl.program_id(1)
    h = h_ref[...]
    off = pl.multiple_of(j * D_MODEL, D_MODEL)
    p_ref[...] = _dot(h, w_ref[:, pl.ds(off, D_MODEL)])

    @pl.when(j == 0)
    def _():
        ab_ref[...] = _dot(h, wab_ref[...])


def _proj(h, w_main, w_ab, *, tm):
    m = h.shape[0]
    return pl.pallas_call(
        _proj_kernel,
        out_shape=(jax.ShapeDtypeStruct((m, N_PROJ_BLOCKS * D_MODEL), F32),
                   jax.ShapeDtypeStruct((m, LANES), F32)),
        grid=(m // tm, N_PROJ_BLOCKS),
        in_specs=[pl.BlockSpec((tm, D_MODEL), lambda i, j: (i, 0)),
                  _resident((D_MODEL, N_PROJ_BLOCKS * D_MODEL)),
                  _resident((D_MODEL, LANES))],
        out_specs=(pl.BlockSpec((tm, D_MODEL), lambda i, j: (i, j)),
                   pl.BlockSpec((tm, LANES), lambda i, j: (i, 0))),
        compiler_params=pltpu.CompilerParams(
            dimension_semantics=("arbitrary", "arbitrary"), vmem_limit_bytes=VMEM_LIMIT),
        name="proj",
    )(h, w_main, w_ab)


HIST = SUBLANES


def _causal_conv(ext_ref, w_ref, taps, G):
    outs = []
    for g in range(G):
        full = ext_ref[g]
        acc = full[HIST:] * w_ref[taps - 1:taps, :]
        for back in range(1, taps):
            shifted = pltpu.roll(full, back, 0)[HIST:]
            acc = acc + shifted * w_ref[taps - 1 - back:taps - back, :]
        outs.append(acc)
    return outs[0] if G == 1 else jnp.concatenate(outs, axis=0)


def _mixer_kernel(scb_ref, scc_ref, scx_ref, qkv_ref, z_ref, ab_ref,
                  sc0_ref, gc0_ref, s0_ref,
                  scw_ref, gcw_ref, alog_ref, dtb_ref, gn_ref,
                  ya_ref, yb_ref, sco_ref, gco_ref, so_ref,
                  ext_sc, ext_gc, *, G, C):
    R = G * C
    c = pl.program_id(1)

    @pl.when(c == 0)
    def _():
        ext_sc[:, HIST - (SC_TAPS - 1):HIST, :] = sc0_ref[...]
        ext_gc[:, HIST - (GDN_TAPS - 1):HIST, :] = gc0_ref[...]
        so_ref[...] = s0_ref[...]

    ext_sc[:, HIST:HIST + C, :] = scc_ref[...] * scx_ref[...]
    conv = _causal_conv(ext_sc, scw_ref, SC_TAPS, G)
    y_a = (scb_ref[...].reshape(R, D_MODEL) * conv).astype(BF16)
    ya_ref[...] = y_a.reshape(ya_ref.shape)
    new_sc = ext_sc[:, HIST + C - (SC_TAPS - 1):HIST + C, :]
    ext_sc[:, HIST - (SC_TAPS - 1):HIST, :] = new_sc
    sco_ref[...] = new_sc

    ext_gc[:, HIST:HIST + C, :] = qkv_ref[...]
    conv = _causal_conv(ext_gc, gcw_ref, GDN_TAPS, G)
    new_gc = ext_gc[:, HIST + C - (GDN_TAPS - 1):HIST + C, :]
    ext_gc[:, HIST - (GDN_TAPS - 1):HIST, :] = new_gc
    gco_ref[...] = new_gc
    qkv = conv * _sigmoid(conv)

    ab = ab_ref[...].reshape(R, LANES)
    xa = ab + dtb_ref[...]
    softplus = jnp.maximum(xa, 0.0) + jnp.log1p(jnp.exp(-jnp.abs(xa)))
    log_a = -jnp.exp(alog_ref[...]) * softplus
    beta_all = _sigmoid(ab)

    row = lax.broadcasted_iota(jnp.int32, (R, R), 0)
    col = lax.broadcasted_iota(jnp.int32, (R, R), 1)
    if G == 1:
        same = None
        incl = row >= col
        strict = row > col
        seg_all = jnp.ones((R, R), BF16)
    else:
        same = (row // C) == (col // C)
        incl = jnp.logical_and(same, row >= col)
        strict = jnp.logical_and(same, row > col)
        seg_all = jnp.where(same, 1.0, 0.0).astype(BF16)
    seg_tri = jnp.where(incl, 1.0, 0.0).astype(BF16)
    eye = jnp.where(row == col, 1.0, 0.0).astype(F32)

    g_col = _dot_exact_lhs(seg_tri, log_a)
    g_last = _dot_exact_lhs(seg_all, log_a)
    g_row = g_col.T
    exp_g = jnp.exp(g_col)
    exp_rest = jnp.exp(g_last - g_col)
    exp_last = jnp.exp(g_last)

    n_sq = max(C.bit_length() - 2, 0)

    heads = range(GDN_HEADS)
    n_key = GDN_HEADS * GDN_DK
    q, k, v, be, eg = [], [], [], [], []
    for h in heads:
        q_h = qkv[:, h * GDN_DK:(h + 1) * GDN_DK]
        k_h = qkv[:, n_key + h * GDN_DK:n_key + (h + 1) * GDN_DK]
        q.append(q_h * (lax.rsqrt(jnp.sum(q_h * q_h, axis=-1, keepdims=True) + L2_EPS)
                        * (GDN_DK ** -0.5)))
        k.append(k_h * lax.rsqrt(jnp.sum(k_h * k_h, axis=-1, keepdims=True) + L2_EPS))
        v.append(qkv[:, 2 * n_key + h * GDN_DV:2 * n_key + (h + 1) * GDN_DV])
        be.append(beta_all[:, GDN_HEADS + h:GDN_HEADS + h + 1])
        eg.append(exp_g[:, h:h + 1])

    a_qk, p, t = [], [], []
    for h in heads:
        dec = jnp.exp(jnp.minimum(g_col[:, h:h + 1] - g_row[h:h + 1, :], 0.0))
        k_b = k[h].astype(BF16)
        kk = _dot_nt(k_b, k_b)
        qk = _dot_nt(q[h].astype(BF16), k_b)
        p.append(jnp.where(strict, -(be[h] * kk * dec), 0.0))
        a_qk.append(jnp.where(incl, qk * dec, 0.0).astype(BF16))
        t.append(eye + p[h])

    for _ in range(n_sq):
        for h in heads:
            p_b = p[h].astype(BF16)
            t_b = t[h].astype(BF16)
            p[h] = _dot(p_b, p_b)
            t[h] = t[h] + _dot(t_b, p[h].astype(BF16))

    u, w = [], []
    for h in heads:
        rhs = jnp.concatenate([be[h] * v[h], (be[h] * eg[h]) * k[h]], axis=-1)
        sol = _dot(t[h].astype(BF16), rhs.astype(BF16))
        u.append(sol[:, :GDN_DV])
        w.append(sol[:, GDN_DV:])

    o = []
    for h in heads:
        q_dec = q[h] * eg[h]
        k_dec = (k[h] * exp_rest[:, h:h + 1]).astype(BF16)
        nus, qss = [], []
        for g in range(G):
            rs = slice(g * C, (g + 1) * C)
            s = so_ref[g, h]
            lhs = jnp.concatenate([w[h][rs], q_dec[rs]], axis=0).astype(BF16)
            r = _dot(lhs, s.astype(BF16))
            nus.append(u[h][rs] - r[:C])
            qss.append(r[C:])
        nu = nus[0] if G == 1 else jnp.concatenate(nus, axis=0)
        qs = qss[0] if G == 1 else jnp.concatenate(qss, axis=0)
        nu_b = nu.astype(BF16)
        o.append(qs + _dot(a_qk[h], nu_b))
        for g in range(G):
            rs = slice(g * C, (g + 1) * C)
            el = exp_last[g * C:g * C + 1, h:h + 1]
            so_ref[g, h] = el * so_ref[g, h] + _dot_tn(k_dec[rs], nu_b[rs])

    for h in heads:
        hs = slice(h * GDN_DV, (h + 1) * GDN_DV)
        on = o[h] * lax.rsqrt(jnp.mean(o[h] * o[h], axis=-1, keepdims=True) + EPS) * gn_ref[...]
        z_h = z_ref[:, :, hs].reshape(R, GDN_DV)
        y_h = (on * (z_h * _sigmoid(z_h))).astype(BF16)
        if len(yb_ref.shape) == 3:
            yb_ref[:, :, hs] = y_h.reshape(G, C, GDN_DV)
        else:
            yb_ref[:, hs] = y_h


def _mixer(proj, ab, sc0, gc0, s0, scw, gcw, alog, dtb, gn, *, G, C, NC):
    R = G * C
    m = proj.shape[0]
    nseq = sc0.shape[0]
    nb = nseq // G
    seq_len = NC * C
    assert m == nseq * seq_len
    proj3 = proj.reshape(nseq, seq_len, proj.shape[1])
    ab3 = ab.reshape(nseq, seq_len, LANES)

    def col(blk, width=D_MODEL):
        return pl.BlockSpec((G, C, width), lambda i, c: (i, c, blk))

    if C % (2 * SUBLANES) == 0:
        y_shape = jax.ShapeDtypeStruct((nseq, seq_len, D_MODEL), BF16)
        y_spec = col(0)
    else:
        assert NC == 1
        y_shape = jax.ShapeDtypeStruct((m, D_MODEL), BF16)
        y_spec = pl.BlockSpec((R, D_MODEL), lambda i, c: (i, 0))
    seq3 = lambda shape: pl.BlockSpec((G,) + shape, lambda i, c: (i, 0, 0))
    state = pl.BlockSpec((G, GDN_HEADS, GDN_DK, GDN_DV), lambda i, c: (i, 0, 0, 0))
    ya, yb, sc, gc, gs = pl.pallas_call(
        functools.partial(_mixer_kernel, G=G, C=C),
        out_shape=(y_shape, y_shape,
                   jax.ShapeDtypeStruct((nseq, SC_TAPS - 1, D_MODEL), F32),
                   jax.ShapeDtypeStruct((nseq, GDN_TAPS - 1, GDN_QKV), F32),
                   jax.ShapeDtypeStruct((nseq, GDN_HEADS, GDN_DK, GDN_DV), F32)),
        grid=(nb, NC),
        in_specs=[col(0), col(1), col(2), col(1, GDN_QKV), col(6),
                  pl.BlockSpec((G, C, LANES), lambda i, c: (i, c, 0)),
                  seq3((SC_TAPS - 1, D_MODEL)), seq3((GDN_TAPS - 1, GDN_QKV)), state,
                  _resident((SC_TAPS, D_MODEL)), _resident((GDN_TAPS, GDN_QKV)),
                  _resident((1, LANES)), _resident((1, LANES)), _resident((1, GDN_DV))],
        out_specs=(y_spec, y_spec,
                   seq3((SC_TAPS - 1, D_MODEL)), seq3((GDN_TAPS - 1, GDN_QKV)), state),
        scratch_shapes=[pltpu.VMEM((G, HIST + C, D_MODEL), F32),
                        pltpu.VMEM((G, HIST + C, GDN_QKV), F32)],
        compiler_params=pltpu.CompilerParams(
            dimension_semantics=("arbitrary", "arbitrary"), vmem_limit_bytes=VMEM_LIMIT),
        name=f"mixer_g{G}_c{C}",
    )(proj3, proj3, proj3, proj3, proj3, ab3, sc0, gc0, s0, scw, gcw, alog, dtb, gn)
    return ya.reshape(m, D_MODEL), yb.reshape(m, D_MODEL), sc, gc, gs


def _merge_kernel(ya_ref, yb_ref, ga_ref, gb_ref, x1_ref, wa_ref, wb_ref, wo_ref, x2_ref):
    merged = (_sigmoid(ga_ref[...]) * _dot(ya_ref[...], wa_ref[...])
              + _sigmoid(gb_ref[...]) * _dot(yb_ref[...], wb_ref[...]))
    x2_ref[...] = x1_ref[...] + _dot(merged.astype(BF16), wo_ref[...])


def _merge(ya, yb, proj, x1, wa, wb, wo, *, tm):
    m = x1.shape[0]
    tile = pl.BlockSpec((tm, D_MODEL), lambda i: (i, 0))
    return pl.pallas_call(
        _merge_kernel,
        out_shape=jax.ShapeDtypeStruct((m, D_MODEL), F32),
        grid=(m // tm,),
        in_specs=[tile, tile,
                  pl.BlockSpec((tm, D_MODEL), lambda i: (i, 7)),
                  pl.BlockSpec((tm, D_MODEL), lambda i: (i, 8)),
                  tile, _resident((D_MODEL, D_MODEL)), _resident((D_MODEL, D_MODEL)),
                  _resident((D_MODEL, D_MODEL))],
        out_specs=tile,
        compiler_params=pltpu.CompilerParams(
            dimension_semantics=("arbitrary",), vmem_limit_bytes=VMEM_LIMIT),
        name="merge",
    )(ya, yb, proj, proj, x1, wa, wb, wo)


FFN_TILE = 512
PROJ_TILE = 1024


def _token_tile(m, tm):
    return tm if m % tm == 0 else m


def kernel(x_prompt, x_sample, state_sconv, state_gdn_conv, state_gdn, meta_tokens, norm_ffn1, ffn1_w_gate, ffn1_w_up, ffn1_w_down, norm_mix, w_in, sconv_w, gdn_conv_w, gdn_a_log, gdn_dt_bias, gdn_norm, w_a_out, w_b_out, w_o, norm_ffn2, ffn2_w_gate, ffn2_w_up, ffn2_w_down, norm_final):
    bsz, seq, _ = x_prompt.shape
    dec_b, dec_len, _ = x_sample.shape
    assert norm_ffn1.shape[0] == 1, "single layer"

    wg1, wu1, wd1 = (w[0].astype(BF16) for w in (ffn1_w_gate, ffn1_w_up, ffn1_w_down))
    wg2, wu2, wd2 = (w[0].astype(BF16) for w in (ffn2_w_gate, ffn2_w_up, ffn2_w_down))
    wa, wb, wo = (w[0].astype(BF16) for w in (w_a_out, w_b_out, w_o))
    n_main = 3 * D_MODEL + GDN_QKV
    n_ab = 2 * GDN_HEADS
    w_in0 = w_in[0].astype(BF16)
    w_main = jnp.concatenate([w_in0[:, :n_main], w_in0[:, n_main + n_ab:]], axis=1)
    w_ab = jnp.pad(w_in0[:, n_main:n_main + n_ab], ((0, 0), (0, LANES - n_ab)))
    row = lambda v: v.reshape(1, -1).astype(F32)
    n1, nm, n2, nf = row(norm_ffn1[0]), row(norm_mix[0]), row(norm_ffn2[0]), row(norm_final)
    pad_heads = lambda v: jnp.pad(v.reshape(1, -1).astype(F32), ((0, 0), (0, LANES - GDN_HEADS)))
    alog, dtb = pad_heads(gdn_a_log[0]), pad_heads(gdn_dt_bias[0])
    gn = row(gdn_norm[0])
    scw, gcw = sconv_w[0].astype(F32), gdn_conv_w[0].astype(F32)
    mix_params = (scw, gcw, alog, dtb, gn)

    def front(x):
        m = x.shape[0]
        x1, h = _ffn(x, n1, wg1, wu1, wd1, nm, final=False, tm=_token_tile(m, FFN_TILE))
        proj, ab = _proj(h, w_main, w_ab, tm=_token_tile(m, PROJ_TILE))
        return x1, proj, ab

    def back(ya, yb, proj, x1):
        m = x1.shape[0]
        x2 = _merge(ya, yb, proj, x1, wa, wb, wo, tm=_token_tile(m, PROJ_TILE))
        return _ffn(x2, n2, wg2, wu2, wd2, nf, final=True, tm=_token_tile(m, FFN_TILE))

    _, proj_m, ab_m = front(meta_tokens.astype(F32))
    zeros = lambda *s: jnp.zeros(s, F32)
    _, _, sc_m, gc_m, s_m = _mixer(
        proj_m, ab_m, zeros(1, SC_TAPS - 1, D_MODEL), zeros(1, GDN_TAPS - 1, GDN_QKV),
        zeros(1, GDN_HEADS, GDN_DK, GDN_DV), *mix_params, G=1, C=N_META, NC=1)

    chunk = 64
    x1_p, proj_p, ab_p = front(x_prompt.reshape(bsz * seq, D_MODEL))
    ya_p, yb_p, p_sc, p_gc, p_gs = _mixer(
        proj_p, ab_p,
        jnp.broadcast_to(sc_m, (bsz,) + sc_m.shape[1:]),
        jnp.broadcast_to(gc_m, (bsz,) + gc_m.shape[1:]),
        jnp.broadcast_to(s_m, (bsz,) + s_m.shape[1:]),
        *mix_params, G=2, C=chunk, NC=seq // chunk)
    y_prompt = back(ya_p, yb_p, proj_p, x1_p).reshape(bsz, seq, D_MODEL)

    x1_s, proj_s, ab_s = front(x_sample.reshape(dec_b * dec_len, D_MODEL))
    ya_s, yb_s, s_sc, s_gc, s_gs = _mixer(
        proj_s, ab_s, state_sconv[0], state_gdn_conv[0], state_gdn[0],
        *mix_params, G=8, C=dec_len, NC=1)
    y_sample = back(ya_s, yb_s, proj_s, x1_s).reshape(dec_b, dec_len, D_MODEL)

    return (y_prompt, y_sample, p_sc[None], p_gc[None], p_gs[None],
            s_sc[None], s_gc[None], s_gs[None])
```

```python
import functools

import jax
import jax.numpy as jnp
from jax import lax
from jax.experimental import pallas as pl
from jax.experimental.pallas import tpu as pltpu

D_MODEL = 1024
D_FF = 2816
N_META = 16
SC_TAPS = 3
GDN_HEADS = 8
GDN_DK = 128
GDN_DV = 128
GDN_QKV = 3 * GDN_HEADS * GDN_DK
GDN_TAPS = 4
EPS = 1e-6
L2_EPS = 1e-6

LANES = 128
SUBLANES = 8
FF_CHUNK = 256
VMEM_LIMIT = 52 * 1024 * 1024

F32 = jnp.float32
BF16 = jnp.bfloat16


def _sigmoid(x):
    return 1.0 / (1.0 + jnp.exp(-x))


def _rms(x, g):
    return x * lax.rsqrt(jnp.mean(x * x, axis=-1, keepdims=True) + EPS) * g


def _dot(a, b):
    return jnp.dot(a, b, preferred_element_type=F32)


def _dot_nt(a, b):
    return lax.dot_general(a, b, (((1,), (1,)), ((), ())), preferred_element_type=F32)


def _dot_tn(a, b):
    return lax.dot_general(a, b, (((0,), (0,)), ((), ())), preferred_element_type=F32)


def _split3(x):
    hi = x.astype(BF16)
    r = x - hi.astype(F32)
    mid = r.astype(BF16)
    lo = (r - mid.astype(F32)).astype(BF16)
    return hi, mid, lo


def _dot_exact_lhs(m_bf16, x):
    hi, mid, lo = _split3(x)
    return _dot(m_bf16, hi) + (_dot(m_bf16, mid) + _dot(m_bf16, lo))


def _resident(shape):
    nd = len(shape)
    return pl.BlockSpec(shape, lambda *_: (0,) * nd, pipeline_mode=pl.Buffered(1))


def _ffn_kernel(x_ref, ng_ref, wg_ref, wu_ref, wd_ref, nn_ref, *rest, final):
    if final:
        (y_ref,) = rest
    else:
        x1_ref, h_ref = rest
    x = x_ref[...]
    h = _rms(x, ng_ref[...]).astype(BF16)
    acc = None
    for f in range(D_FF // FF_CHUNK):
        cols = slice(f * FF_CHUNK, (f + 1) * FF_CHUNK)
        g = _dot(h, wg_ref[:, cols])
        u = _dot(h, wu_ref[:, cols])
        a = (g * _sigmoid(g) * u).astype(BF16)
        d = _dot(a, wd_ref[cols, :])
        acc = d if acc is None else acc + d
    x1 = x + 0.5 * acc
    if final:
        y_ref[...] = _rms(x1, nn_ref[...])
    else:
        x1_ref[...] = x1
        h_ref[...] = _rms(x1, nn_ref[...]).astype(BF16)


def _ffn(x, ng, wg, wu, wd, nn, *, final, tm):
    m = x.shape[0]
    tile = pl.BlockSpec((tm, D_MODEL), lambda i: (i, 0))
    if final:
        out_shape = jax.ShapeDtypeStruct((m, D_MODEL), F32)
        out_specs = tile
    else:
        out_shape = (jax.ShapeDtypeStruct((m, D_MODEL), F32),
                     jax.ShapeDtypeStruct((m, D_MODEL), BF16))
        out_specs = (tile, tile)
    return pl.pallas_call(
        functools.partial(_ffn_kernel, final=final),
        out_shape=out_shape,
        grid=(m // tm,),
        in_specs=[tile, _resident((1, D_MODEL)), _resident((D_MODEL, D_FF)),
                  _resident((D_MODEL, D_FF)), _resident((D_FF, D_MODEL)),
                  _resident((1, D_MODEL))],
        out_specs=out_specs,
        compiler_params=pltpu.CompilerParams(
            dimension_semantics=("arbitrary",), vmem_limit_bytes=VMEM_LIMIT),
        name="ffn_final" if final else "ffn_mid",
    )(x, ng, wg, wu, wd, nn)


N_PROJ_BLOCKS = 9


def _proj_kernel(h_ref, w_ref, wab_ref, p_ref, ab_ref):
    j = pl.program_id(1)
    h = h_ref[...]
    off = pl.multiple_of(j * D_MODEL, D_MODEL)
    p_ref[...] = _dot(h, w_ref[:, pl.ds(off, D_MODEL)])

    @pl.when(j == 0)
    def _():
        ab_ref[...] = _dot(h, wab_ref[...])


def _proj(h, w_main, w_ab, *, tm):
    m = h.shape[0]
    return pl.pallas_call(
        _proj_kernel,
        out_shape=(jax.ShapeDtypeStruct((m, N_PROJ_BLOCKS * D_MODEL), F32),
                   jax.ShapeDtypeStruct((m, LANES), F32)),
        grid=(m // tm, N_PROJ_BLOCKS),
        in_specs=[pl.BlockSpec((tm, D_MODEL), lambda i, j: (i, 0)),
                  _resident((D_MODEL, N_PROJ_BLOCKS * D_MODEL)),
                  _resident((D_MODEL, LANES))],
        out_specs=(pl.BlockSpec((tm, D_MODEL), lambda i, j: (i, j)),
                   pl.BlockSpec((tm, LANES), lambda i, j: (i, 0))),
        compiler_params=pltpu.CompilerParams(
            dimension_semantics=("arbitrary", "arbitrary"), vmem_limit_bytes=VMEM_LIMIT),
        name="proj",
    )(h, w_main, w_ab)


HIST = SUBLANES


def _ext_store(ext_ref, lo, val):
    for lt in range(ext_ref.shape[1]):
        ext_ref[:, lt, lo:lo + val.shape[1], :] = val[:, :, lt * LANES:(lt + 1) * LANES]


def _ext_load(ext_ref, lo, rows):
    return jnp.concatenate([ext_ref[:, lt, lo:lo + rows, :] for lt in range(ext_ref.shape[1])],
                           axis=-1)


def _causal_conv(ext_ref, w_ref, taps, G, C, out_ref, post=None):
    n_lt = ext_ref.shape[1]
    if C % (SUBLANES * SUBLANES) != 0:
        full = _ext_load(ext_ref, 0, HIST + C)
        outs = []
        for g in range(G):
            acc = full[g, HIST:] * w_ref[taps - 1:taps, :]
            for back in range(1, taps):
                shifted = pltpu.roll(full[g], back, 0)[HIST:]
                acc = acc + shifted * w_ref[taps - 1 - back:taps - back, :]
            outs.append(acc if post is None else post(acc))
        return outs[0] if G == 1 else jnp.concatenate(outs, axis=0)
    n = C // SUBLANES
    for g in range(G):
        for lt in range(n_lt):
            cs = slice(lt * LANES, (lt + 1) * LANES)
            slab = {d: ext_ref[g, lt, pl.ds(HIST + d, n, stride=SUBLANES), :]
                    for d in range(1 - taps, SUBLANES)}
            for a in range(SUBLANES):
                acc = slab[a] * w_ref[taps - 1:taps, cs]
                for back in range(1, taps):
                    acc = acc + slab[a - back] * w_ref[taps - 1 - back:taps - back, cs]
                out_ref[g, lt, pl.ds(a, n, stride=SUBLANES), :] = (
                    acc if post is None else post(acc))
    return _ext_load(out_ref, 0, C).reshape(G * C, n_lt * LANES)


def _mixer_kernel(scb_ref, scc_ref, scx_ref, qkv_ref, z_ref, ab_ref,
                  sc0_ref, gc0_ref, s0_ref,
                  scw_ref, gcw_ref, alog_ref, dtb_ref, gn_ref,
                  ya_ref, yb_ref, sco_ref, gco_ref, so_ref,
                  ext_sc, ext_gc, conv_sc, conv_gc, *, G, C):
    R = G * C
    c = pl.program_id(1)

    @pl.when(c == 0)
    def _():
        _ext_store(ext_sc, HIST - (SC_TAPS - 1), sc0_ref[...])
        _ext_store(ext_gc, HIST - (GDN_TAPS - 1), gc0_ref[...])
        so_ref[...] = s0_ref[...]

    _ext_store(ext_sc, HIST, scc_ref[...] * scx_ref[...])
    conv = _causal_conv(ext_sc, scw_ref, SC_TAPS, G, C, conv_sc)
    y_a = (scb_ref[...].reshape(R, D_MODEL) * conv).astype(BF16)
    ya_ref[...] = y_a.reshape(ya_ref.shape)
    new_sc = _ext_load(ext_sc, HIST + C - (SC_TAPS - 1), SC_TAPS - 1)
    _ext_store(ext_sc, HIST - (SC_TAPS - 1), new_sc)
    sco_ref[...] = new_sc

    _ext_store(ext_gc, HIST, qkv_ref[...])
    qkv = _causal_conv(ext_gc, gcw_ref, GDN_TAPS, G, C, conv_gc, post=lambda x: x * _sigmoid(x))
    new_gc = _ext_load(ext_gc, HIST + C - (GDN_TAPS - 1), GDN_TAPS - 1)
    _ext_store(ext_gc, HIST - (GDN_TAPS - 1), new_gc)
    gco_ref[...] = new_gc

    ab = ab_ref[...].reshape(R, LANES)
    xa = ab + dtb_ref[...]
    softplus = jnp.maximum(xa, 0.0) + jnp.log1p(jnp.exp(-jnp.abs(xa)))
    log_a = -jnp.exp(alog_ref[...]) * softplus
    beta_all = _sigmoid(ab)

    row = lax.broadcasted_iota(jnp.int32, (R, R), 0)
    col = lax.broadcasted_iota(jnp.int32, (R, R), 1)
    if G == 1:
        incl = row >= col
        strict = row > col
        seg_all = jnp.ones((R, R), BF16)
    else:
        same = (row // C) == (col // C)
        incl = jnp.logical_and(same, row >= col)
        strict = jnp.logical_and(same, row > col)
        seg_all = jnp.where(same, 1.0, 0.0).astype(BF16)
    seg_tri = jnp.where(incl, 1.0, 0.0).astype(BF16)
    eye = jnp.where(row == col, 1.0, 0.0).astype(F32)

    g_col = _dot_exact_lhs(seg_tri, log_a)
    g_last = _dot_exact_lhs(seg_all, log_a)
    g_row = g_col.T
    exp_g = jnp.exp(g_col)
    exp_rest = jnp.exp(g_last - g_col)
    exp_last = jnp.exp(g_last)

    n_sq = max(C.bit_length() - 2, 0)

    heads = range(GDN_HEADS)
    n_key = GDN_HEADS * GDN_DK
    q, k, v, be, eg = [], [], [], [], []
    for h in heads:
        q_h = qkv[:, h * GDN_DK:(h + 1) * GDN_DK]
        k_h = qkv[:, n_key + h * GDN_DK:n_key + (h + 1) * GDN_DK]
        q.append(q_h * (lax.rsqrt(jnp.sum(q_h * q_h, axis=-1, keepdims=True) + L2_EPS)
                        * (GDN_DK ** -0.5)))
        k.append(k_h * lax.rsqrt(jnp.sum(k_h * k_h, axis=-1, keepdims=True) + L2_EPS))
        v.append(qkv[:, 2 * n_key + h * GDN_DV:2 * n_key + (h + 1) * GDN_DV])
        be.append(beta_all[:, GDN_HEADS + h:GDN_HEADS + h + 1])
        eg.append(exp_g[:, h:h + 1])

    a_qk, p, t = [], [], []
    for h in heads:
        dec = jnp.exp(jnp.minimum(g_col[:, h:h + 1] - g_row[h:h + 1, :], 0.0))
        k_b = k[h].astype(BF16)
        kk = _dot_nt(k_b, k_b)
        qk = _dot_nt(q[h].astype(BF16), k_b)
        p.append(jnp.where(strict, -(be[h] * kk * dec), 0.0))
        a_qk.append(jnp.where(incl, qk * dec, 0.0).astype(BF16))
        t.append(eye + p[h])

    for _ in range(n_sq):
        for h in heads:
            p_b = p[h].astype(BF16)
            t_b = t[h].astype(BF16)
            p[h] = _dot(p_b, p_b)
            t[h] = t[h] + _dot(t_b, p[h].astype(BF16))

    u, w = [], []
    for h in heads:
        rhs = jnp.concatenate([be[h] * v[h], (be[h] * eg[h]) * k[h]], axis=-1)
        sol = _dot(t[h].astype(BF16), rhs.astype(BF16))
        u.append(sol[:, :GDN_DV])
        w.append(sol[:, GDN_DV:])

    o = []
    for h in heads:
        q_dec = q[h] * eg[h]
        k_dec = (k[h] * exp_rest[:, h:h + 1]).astype(BF16)
        nus, qss = [], []
        for g in range(G):
            rs = slice(g * C, (g + 1) * C)
            s = so_ref[g, h]
            lhs = jnp.concatenate([w[h][rs], q_dec[rs]], axis=0).astype(BF16)
            r = _dot(lhs, s.astype(BF16))
            nus.append(u[h][rs] - r[:C])
            qss.append(r[C:])
        nu = nus[0] if G == 1 else jnp.concatenate(nus, axis=0)
        qs = qss[0] if G == 1 else jnp.concatenate(qss, axis=0)
        nu_b = nu.astype(BF16)
        o.append(qs + _dot(a_qk[h], nu_b))
        for g in range(G):
            rs = slice(g * C, (g + 1) * C)
            el = exp_last[g * C:g * C + 1, h:h + 1]
            so_ref[g, h] = el * so_ref[g, h] + _dot_tn(k_dec[rs], nu_b[rs])

    for h in heads:
        hs = slice(h * GDN_DV, (h + 1) * GDN_DV)
        on = o[h] * lax.rsqrt(jnp.mean(o[h] * o[h], axis=-1, keepdims=True) + EPS) * gn_ref[...]
        z_h = z_ref[:, :, hs].reshape(R, GDN_DV)
        y_h = (on * (z_h * _sigmoid(z_h))).astype(BF16)
        if len(yb_ref.shape) == 3:
            yb_ref[:, :, hs] = y_h.reshape(G, C, GDN_DV)
        else:
            yb_ref[:, hs] = y_h


def _mixer(proj, ab, sc0, gc0, s0, scw, gcw, alog, dtb, gn, *, G, C, NC):
    R = G * C
    m = proj.shape[0]
    nseq = sc0.shape[0]
    nb = nseq // G
    seq_len = NC * C
    assert m == nseq * seq_len
    proj3 = proj.reshape(nseq, seq_len, proj.shape[1])
    ab3 = ab.reshape(nseq, seq_len, LANES)

    def col(blk, width=D_MODEL):
        return pl.BlockSpec((G, C, width), lambda i, c: (i, c, blk))

    if C % (2 * SUBLANES) == 0:
        y_shape = jax.ShapeDtypeStruct((nseq, seq_len, D_MODEL), BF16)
        y_spec = col(0)
    else:
        assert NC == 1
        y_shape = jax.ShapeDtypeStruct((m, D_MODEL), BF16)
        y_spec = pl.BlockSpec((R, D_MODEL), lambda i, c: (i, 0))
    seq3 = lambda shape: pl.BlockSpec((G,) + shape, lambda i, c: (i, 0, 0))
    state = pl.BlockSpec((G, GDN_HEADS, GDN_DK, GDN_DV), lambda i, c: (i, 0, 0, 0))
    ya, yb, sc, gc, gs = pl.pallas_call(
        functools.partial(_mixer_kernel, G=G, C=C),
        out_shape=(y_shape, y_shape,
                   jax.ShapeDtypeStruct((nseq, SC_TAPS - 1, D_MODEL), F32),
                   jax.ShapeDtypeStruct((nseq, GDN_TAPS - 1, GDN_QKV), F32),
                   jax.ShapeDtypeStruct((nseq, GDN_HEADS, GDN_DK, GDN_DV), F32)),
        grid=(nb, NC),
        in_specs=[col(0), col(1), col(2), col(1, GDN_QKV), col(6),
                  pl.BlockSpec((G, C, LANES), lambda i, c: (i, c, 0)),
                  seq3((SC_TAPS - 1, D_MODEL)), seq3((GDN_TAPS - 1, GDN_QKV)), state,
                  _resident((SC_TAPS, D_MODEL)), _resident((GDN_TAPS, GDN_QKV)),
                  _resident((1, LANES)), _resident((1, LANES)), _resident((1, GDN_DV))],
        out_specs=(y_spec, y_spec,
                   seq3((SC_TAPS - 1, D_MODEL)), seq3((GDN_TAPS - 1, GDN_QKV)), state),
        scratch_shapes=[pltpu.VMEM((G, D_MODEL // LANES, HIST + C, LANES), F32),
                        pltpu.VMEM((G, GDN_QKV // LANES, HIST + C, LANES), F32),
                        pltpu.VMEM((G, D_MODEL // LANES, C, LANES), F32),
                        pltpu.VMEM((G, GDN_QKV // LANES, C, LANES), F32)],
        compiler_params=pltpu.CompilerParams(
            dimension_semantics=("arbitrary", "arbitrary"), vmem_limit_bytes=VMEM_LIMIT),
        name=f"mixer_g{G}_c{C}",
    )(proj3, proj3, proj3, proj3, proj3, ab3, sc0, gc0, s0, scw, gcw, alog, dtb, gn)
    return ya.reshape(m, D_MODEL), yb.reshape(m, D_MODEL), sc, gc, gs


def _merge_kernel(ya_ref, yb_ref, ga_ref, gb_ref, x1_ref, wa_ref, wb_ref, wo_ref, x2_ref):
    merged = (_sigmoid(ga_ref[...]) * _dot(ya_ref[...], wa_ref[...])
              + _sigmoid(gb_ref[...]) * _dot(yb_ref[...], wb_ref[...]))
    x2_ref[...] = x1_ref[...] + _dot(merged.astype(BF16), wo_ref[...])


def _merge(ya, yb, proj, x1, wa, wb, wo, *, tm):
    m = x1.shape[0]
    tile = pl.BlockSpec((tm, D_MODEL), lambda i: (i, 0))
    return pl.pallas_call(
        _merge_kernel,
        out_shape=jax.ShapeDtypeStruct((m, D_MODEL), F32),
        grid=(m // tm,),
        in_specs=[tile, tile,
                  pl.BlockSpec((tm, D_MODEL), lambda i: (i, 7)),
                  pl.BlockSpec((tm, D_MODEL), lambda i: (i, 8)),
                  tile, _resident((D_MODEL, D_MODEL)), _resident((D_MODEL, D_MODEL)),
                  _resident((D_MODEL, D_MODEL))],
        out_specs=tile,
        compiler_params=pltpu.CompilerParams(
            dimension_semantics=("arbitrary",), vmem_limit_bytes=VMEM_LIMIT),
        name="merge",
    )(ya, yb, proj, proj, x1, wa, wb, wo)


FFN_TILE = 512
PROJ_TILE = 1024


def _token_tile(m, tm):
    return tm if m % tm == 0 else m


def kernel(x_prompt, x_sample, state_sconv, state_gdn_conv, state_gdn, meta_tokens, norm_ffn1, ffn1_w_gate, ffn1_w_up, ffn1_w_down, norm_mix, w_in, sconv_w, gdn_conv_w, gdn_a_log, gdn_dt_bias, gdn_norm, w_a_out, w_b_out, w_o, norm_ffn2, ffn2_w_gate, ffn2_w_up, ffn2_w_down, norm_final):
    bsz, seq, _ = x_prompt.shape
    dec_b, dec_len, _ = x_sample.shape
    assert norm_ffn1.shape[0] == 1, "single layer"

    wg1, wu1, wd1 = (w[0].astype(BF16) for w in (ffn1_w_gate, ffn1_w_up, ffn1_w_down))
    wg2, wu2, wd2 = (w[0].astype(BF16) for w in (ffn2_w_gate, ffn2_w_up, ffn2_w_down))
    wa, wb, wo = (w[0].astype(BF16) for w in (w_a_out, w_b_out, w_o))
    n_main = 3 * D_MODEL + GDN_QKV
    n_ab = 2 * GDN_HEADS
    w_in0 = w_in[0].astype(BF16)
    w_main = jnp.concatenate([w_in0[:, :n_main], w_in0[:, n_main + n_ab:]], axis=1)
    w_ab = jnp.pad(w_in0[:, n_main:n_main + n_ab], ((0, 0), (0, LANES - n_ab)))
    row = lambda v: v.reshape(1, -1).astype(F32)
    n1, nm, n2, nf = row(norm_ffn1[0]), row(norm_mix[0]), row(norm_ffn2[0]), row(norm_final)
    pad_heads = lambda v: jnp.pad(v.reshape(1, -1).astype(F32), ((0, 0), (0, LANES - GDN_HEADS)))
    alog, dtb = pad_heads(gdn_a_log[0]), pad_heads(gdn_dt_bias[0])
    gn = row(gdn_norm[0])
    scw, gcw = sconv_w[0].astype(F32), gdn_conv_w[0].astype(F32)
    mix_params = (scw, gcw, alog, dtb, gn)

    def front(x):
        m = x.shape[0]
        x1, h = _ffn(x, n1, wg1, wu1, wd1, nm, final=False, tm=_token_tile(m, FFN_TILE))
        proj, ab = _proj(h, w_main, w_ab, tm=_token_tile(m, PROJ_TILE))
        return x1, proj, ab

    def back(ya, yb, proj, x1):
        m = x1.shape[0]
        x2 = _merge(ya, yb, proj, x1, wa, wb, wo, tm=_token_tile(m, PROJ_TILE))
        return _ffn(x2, n2, wg2, wu2, wd2, nf, final=True, tm=_token_tile(m, FFN_TILE))

    _, proj_m, ab_m = front(meta_tokens.astype(F32))
    zeros = lambda *s: jnp.zeros(s, F32)
    _, _, sc_m, gc_m, s_m = _mixer(
        proj_m, ab_m, zeros(1, SC_TAPS - 1, D_MODEL), zeros(1, GDN_TAPS - 1, GDN_QKV),
        zeros(1, GDN_HEADS, GDN_DK, GDN_DV), *mix_params, G=1, C=N_META, NC=1)

    chunk = 64
    x1_p, proj_p, ab_p = front(x_prompt.reshape(bsz * seq, D_MODEL))
    ya_p, yb_p, p_sc, p_gc, p_gs = _mixer(
        proj_p, ab_p,
        jnp.broadcast_to(sc_m, (bsz,) + sc_m.shape[1:]),
        jnp.broadcast_to(gc_m, (bsz,) + gc_m.shape[1:]),
        jnp.broadcast_to(s_m, (bsz,) + s_m.shape[1:]),
        *mix_params, G=2, C=chunk, NC=seq // chunk)
    y_prompt = back(ya_p, yb_p, proj_p, x1_p).reshape(bsz, seq, D_MODEL)

    x1_s, proj_s, ab_s = front(x_sample.reshape(dec_b * dec_len, D_MODEL))
    ya_s, yb_s, s_sc, s_gc, s_gs = _mixer(
        proj_s, ab_s, state_sconv[0], state_gdn_conv[0], state_gdn[0],
        *mix_params, G=8, C=dec_len, NC=1)
    y_sample = back(ya_s, yb_s, proj_s, x1_s).reshape(dec_b, dec_len, D_MODEL)

    return (y_prompt, y_sample, p_sc[None], p_gc[None], p_gs[None],
            s_sc[None], s_gc[None], s_gs[None])
```

```python
import functools

import jax
import jax.numpy as jnp
from jax import lax
from jax.experimental import pallas as pl
from jax.experimental.pallas import tpu as pltpu

D_MODEL = 1024
D_FF = 2816
N_META = 16
SC_TAPS = 3
GDN_HEADS = 8
GDN_DK = 128
GDN_DV = 128
GDN_QKV = 3 * GDN_HEADS * GDN_DK
GDN_TAPS = 4
EPS = 1e-6
L2_EPS = 1e-6

LANES = 128
SUBLANES = 8
FF_CHUNK = 256
VMEM_LIMIT = 52 * 1024 * 1024

F32 = jnp.float32
BF16 = jnp.bfloat16


def _sigmoid(x):
    return 1.0 / (1.0 + jnp.exp(-x))


def _rms(x, g):
    return x * lax.rsqrt(jnp.mean(x * x, axis=-1, keepdims=True) + EPS) * g


def _dot(a, b):
    return jnp.dot(a, b, preferred_element_type=F32)


def _dot_nt(a, b):
    return lax.dot_general(a, b, (((1,), (1,)), ((), ())), preferred_element_type=F32)


def _dot_tn(a, b):
    return lax.dot_general(a, b, (((0,), (0,)), ((), ())), preferred_element_type=F32)


def _split3(x):
    hi = x.astype(BF16)
    r = x - hi.astype(F32)
    mid = r.astype(BF16)
    lo = (r - mid.astype(F32)).astype(BF16)
    return hi, mid, lo


def _dot_exact_lhs(m_bf16, x):
    hi, mid, lo = _split3(x)
    return _dot(m_bf16, hi) + (_dot(m_bf16, mid) + _dot(m_bf16, lo))


def _resident(shape):
    nd = len(shape)
    return pl.BlockSpec(shape, lambda *_: (0,) * nd, pipeline_mode=pl.Buffered(1))


def _ffn_kernel(x_ref, ng_ref, wg_ref, wu_ref, wd_ref, nn_ref, *rest, final):
    if final:
        (y_ref,) = rest
    else:
        x1_ref, h_ref = rest
    x = x_ref[...]
    h = _rms(x, ng_ref[...]).astype(BF16)
    acc = None
    for f in range(D_FF // FF_CHUNK):
        cols = slice(f * FF_CHUNK, (f + 1) * FF_CHUNK)
        g = _dot(h, wg_ref[:, cols])
        u = _dot(h, wu_ref[:, cols])
        a = (g * _sigmoid(g) * u).astype(BF16)
        d = _dot(a, wd_ref[cols, :])
        acc = d if acc is None else acc + d
    x1 = x + 0.5 * acc
    if final:
        y_ref[...] = _rms(x1, nn_ref[...])
    else:
        x1_ref[...] = x1
        h_ref[...] = _rms(x1, nn_ref[...]).astype(BF16)


def _ffn(x, ng, wg, wu, wd, nn, *, final, tm):
    m = x.shape[0]
    tile = pl.BlockSpec((tm, D_MODEL), lambda i: (i, 0))
    if final:
        out_shape = jax.ShapeDtypeStruct((m, D_MODEL), F32)
        out_specs = tile
    else:
        out_shape = (jax.ShapeDtypeStruct((m, D_MODEL), F32),
                     jax.ShapeDtypeStruct((m, D_MODEL), BF16))
        out_specs = (tile, tile)
    return pl.pallas_call(
        functools.partial(_ffn_kernel, final=final),
        out_shape=out_shape,
        grid=(m // tm,),
        in_specs=[tile, _resident((1, D_MODEL)), _resident((D_MODEL, D_FF)),
                  _resident((D_MODEL, D_FF)), _resident((D_FF, D_MODEL)),
                  _resident((1, D_MODEL))],
        out_specs=out_specs,
        compiler_params=pltpu.CompilerParams(
            dimension_semantics=("arbitrary",), vmem_limit_bytes=VMEM_LIMIT),
        name="ffn_final" if final else "ffn_mid",
    )(x, ng, wg, wu, wd, nn)


N_PROJ_BLOCKS = 9
PROJ_STEP_COLS = 3 * D_MODEL


def _proj_kernel(h_ref, w_ref, wab_ref, p_ref, ab_ref):
    j = pl.program_id(1)
    h = h_ref[...]
    off = pl.multiple_of(j * PROJ_STEP_COLS, PROJ_STEP_COLS)
    p_ref[...] = _dot(h, w_ref[:, pl.ds(off, PROJ_STEP_COLS)])

    @pl.when(j == 0)
    def _():
        ab_ref[...] = _dot(h, wab_ref[...])


def _proj(h, w_main, w_ab, *, tm):
    m = h.shape[0]
    n_cols = N_PROJ_BLOCKS * D_MODEL
    return pl.pallas_call(
        _proj_kernel,
        out_shape=(jax.ShapeDtypeStruct((m, n_cols), F32),
                   jax.ShapeDtypeStruct((m, LANES), F32)),
        grid=(m // tm, n_cols // PROJ_STEP_COLS),
        in_specs=[pl.BlockSpec((tm, D_MODEL), lambda i, j: (i, 0)),
                  _resident((D_MODEL, n_cols)),
                  _resident((D_MODEL, LANES))],
        out_specs=(pl.BlockSpec((tm, PROJ_STEP_COLS), lambda i, j: (i, j)),
                   pl.BlockSpec((tm, LANES), lambda i, j: (i, 0))),
        compiler_params=pltpu.CompilerParams(
            dimension_semantics=("arbitrary", "arbitrary"), vmem_limit_bytes=VMEM_LIMIT),
        name="proj",
    )(h, w_main, w_ab)


HIST = SUBLANES


def _ext_store(ext_ref, lo, val):
    for lt in range(ext_ref.shape[1]):
        ext_ref[:, lt, lo:lo + val.shape[1], :] = val[:, :, lt * LANES:(lt + 1) * LANES]


def _ext_load(ext_ref, lo, rows):
    return jnp.concatenate([ext_ref[:, lt, lo:lo + rows, :] for lt in range(ext_ref.shape[1])],
                           axis=-1)


def _causal_conv(ext_ref, w_ref, taps, G, C, out_ref, post=None):
    n_lt = ext_ref.shape[1]
    if C % (SUBLANES * SUBLANES) != 0:
        full = _ext_load(ext_ref, 0, HIST + C)
        outs = []
        for g in range(G):
            acc = full[g, HIST:] * w_ref[taps - 1:taps, :]
            for back in range(1, taps):
                shifted = pltpu.roll(full[g], back, 0)[HIST:]
                acc = acc + shifted * w_ref[taps - 1 - back:taps - back, :]
            outs.append(acc if post is None else post(acc))
        return outs[0] if G == 1 else jnp.concatenate(outs, axis=0)
    n = C // SUBLANES
    for g in range(G):
        for lt in range(n_lt):
            cs = slice(lt * LANES, (lt + 1) * LANES)
            slab = {d: ext_ref[g, lt, pl.ds(HIST + d, n, stride=SUBLANES), :]
                    for d in range(1 - taps, SUBLANES)}
            for a in range(SUBLANES):
                acc = slab[a] * w_ref[taps - 1:taps, cs]
                for back in range(1, taps):
                    acc = acc + slab[a - back] * w_ref[taps - 1 - back:taps - back, cs]
                out_ref[g, lt, pl.ds(a, n, stride=SUBLANES), :] = (
                    acc if post is None else post(acc))
    return _ext_load(out_ref, 0, C).reshape(G * C, n_lt * LANES)


def _mixer_kernel(scb_ref, scc_ref, scx_ref, qkv_ref, z_ref, ab_ref,
                  sc0_ref, gc0_ref, s0_ref,
                  scw_ref, gcw_ref, alog_ref, dtb_ref, gn_ref,
                  ya_ref, yb_ref, sco_ref, gco_ref, so_ref,
                  ext_sc, ext_gc, conv_sc, conv_gc, *, G, C):
    R = G * C
    c = pl.program_id(1)

    @pl.when(c == 0)
    def _():
        _ext_store(ext_sc, HIST - (SC_TAPS - 1), sc0_ref[...])
        _ext_store(ext_gc, HIST - (GDN_TAPS - 1), gc0_ref[...])
        so_ref[...] = s0_ref[...]

    _ext_store(ext_sc, HIST, scc_ref[...] * scx_ref[...])
    conv = _causal_conv(ext_sc, scw_ref, SC_TAPS, G, C, conv_sc)
    y_a = (scb_ref[...].reshape(R, D_MODEL) * conv).astype(BF16)
    ya_ref[...] = y_a.reshape(ya_ref.shape)
    new_sc = _ext_load(ext_sc, HIST + C - (SC_TAPS - 1), SC_TAPS - 1)
    _ext_store(ext_sc, HIST - (SC_TAPS - 1), new_sc)
    sco_ref[...] = new_sc

    _ext_store(ext_gc, HIST, qkv_ref[...])
    qkv = _causal_conv(ext_gc, gcw_ref, GDN_TAPS, G, C, conv_gc, post=lambda x: x * _sigmoid(x))
    new_gc = _ext_load(ext_gc, HIST + C - (GDN_TAPS - 1), GDN_TAPS - 1)
    _ext_store(ext_gc, HIST - (GDN_TAPS - 1), new_gc)
    gco_ref[...] = new_gc

    ab = ab_ref[...].reshape(R, LANES)
    xa = ab + dtb_ref[...]
    softplus = jnp.maximum(xa, 0.0) + jnp.log1p(jnp.exp(-jnp.abs(xa)))
    log_a = -jnp.exp(alog_ref[...]) * softplus
    beta_all = _sigmoid(ab)

    row = lax.broadcasted_iota(jnp.int32, (R, R), 0)
    col = lax.broadcasted_iota(jnp.int32, (R, R), 1)
    if G == 1:
        incl = row >= col
        strict = row > col
        seg_all = jnp.ones((R, R), BF16)
    else:
        same = (row // C) == (col // C)
        incl = jnp.logical_and(same, row >= col)
        strict = jnp.logical_and(same, row > col)
        seg_all = jnp.where(same, 1.0, 0.0).astype(BF16)
    seg_tri = jnp.where(incl, 1.0, 0.0).astype(BF16)
    eye = jnp.where(row == col, 1.0, 0.0).astype(F32)

    g_col = _dot_exact_lhs(seg_tri, log_a)
    g_last = _dot_exact_lhs(seg_all, log_a)
    g_row = g_col.T
    exp_g = jnp.exp(g_col)
    exp_rest = jnp.exp(g_last - g_col)
    exp_last = jnp.exp(g_last)

    n_sq = max(C.bit_length() - 2, 0)

    heads = range(GDN_HEADS)
    n_key = GDN_HEADS * GDN_DK
    q, k, v, be, eg = [], [], [], [], []
    for h in heads:
        q_h = qkv[:, h * GDN_DK:(h + 1) * GDN_DK]
        k_h = qkv[:, n_key + h * GDN_DK:n_key + (h + 1) * GDN_DK]
        q.append(q_h * (lax.rsqrt(jnp.sum(q_h * q_h, axis=-1, keepdims=True) + L2_EPS)
                        * (GDN_DK ** -0.5)))
        k.append(k_h * lax.rsqrt(jnp.sum(k_h * k_h, axis=-1, keepdims=True) + L2_EPS))
        v.append(qkv[:, 2 * n_key + h * GDN_DV:2 * n_key + (h + 1) * GDN_DV])
        be.append(beta_all[:, GDN_HEADS + h:GDN_HEADS + h + 1])
        eg.append(exp_g[:, h:h + 1])

    a_qk, p, t = [], [], []
    for h in heads:
        dec = jnp.exp(jnp.minimum(g_col[:, h:h + 1] - g_row[h:h + 1, :], 0.0))
        k_b = k[h].astype(BF16)
        kk = _dot_nt(k_b, k_b)
        qk = _dot_nt(q[h].astype(BF16), k_b)
        p.append(jnp.where(strict, -(be[h] * kk * dec), 0.0))
        a_qk.append(jnp.where(incl, qk * dec, 0.0).astype(BF16))
        t.append(eye + p[h])

    for _ in range(n_sq):
        for h in heads:
            p_b = p[h].astype(BF16)
            t_b = t[h].astype(BF16)
            p[h] = _dot(p_b, p_b)
            t[h] = t[h] + _dot(t_b, p[h].astype(BF16))

    u, w = [], []
    for h in heads:
        rhs = jnp.concatenate([be[h] * v[h], (be[h] * eg[h]) * k[h]], axis=-1)
        sol = _dot(t[h].astype(BF16), rhs.astype(BF16))
        u.append(sol[:, :GDN_DV])
        w.append(sol[:, GDN_DV:])

    o = []
    for h in heads:
        q_dec = q[h] * eg[h]
        k_dec = (k[h] * exp_rest[:, h:h + 1]).astype(BF16)
        nus, qss = [], []
        for g in range(G):
            rs = slice(g * C, (g + 1) * C)
            s = so_ref[g, h]
            lhs = jnp.concatenate([w[h][rs], q_dec[rs]], axis=0).astype(BF16)
            r = _dot(lhs, s.astype(BF16))
            nus.append(u[h][rs] - r[:C])
            qss.append(r[C:])
        nu = nus[0] if G == 1 else jnp.concatenate(nus, axis=0)
        qs = qss[0] if G == 1 else jnp.concatenate(qss, axis=0)
        nu_b = nu.astype(BF16)
        o.append(qs + _dot(a_qk[h], nu_b))
        for g in range(G):
            rs = slice(g * C, (g + 1) * C)
            el = exp_last[g * C:g * C + 1, h:h + 1]
            so_ref[g, h] = el * so_ref[g, h] + _dot_tn(k_dec[rs], nu_b[rs])

    for h in heads:
        hs = slice(h * GDN_DV, (h + 1) * GDN_DV)
        on = o[h] * lax.rsqrt(jnp.mean(o[h] * o[h], axis=-1, keepdims=True) + EPS) * gn_ref[...]
        z_h = z_ref[:, :, hs].reshape(R, GDN_DV)
        y_h = (on * (z_h * _sigmoid(z_h))).astype(BF16)
        if len(yb_ref.shape) == 3:
            yb_ref[:, :, hs] = y_h.reshape(G, C, GDN_DV)
        else:
            yb_ref[:, hs] = y_h


def _mixer(proj, ab, sc0, gc0, s0, scw, gcw, alog, dtb, gn, *, G, C, NC):
    R = G * C
    m = proj.shape[0]
    nseq = sc0.shape[0]
    nb = nseq // G
    seq_len = NC * C
    assert m == nseq * seq_len
    proj3 = proj.reshape(nseq, seq_len, proj.shape[1])
    ab3 = ab.reshape(nseq, seq_len, LANES)

    def col(blk, width=D_MODEL):
        return pl.BlockSpec((G, C, width), lambda i, c: (i, c, blk))

    if C % (2 * SUBLANES) == 0:
        y_shape = jax.ShapeDtypeStruct((nseq, seq_len, D_MODEL), BF16)
        y_spec = col(0)
    else:
        assert NC == 1
        y_shape = jax.ShapeDtypeStruct((m, D_MODEL), BF16)
        y_spec = pl.BlockSpec((R, D_MODEL), lambda i, c: (i, 0))
    seq3 = lambda shape: pl.BlockSpec((G,) + shape, lambda i, c: (i, 0, 0))
    state = pl.BlockSpec((G, GDN_HEADS, GDN_DK, GDN_DV), lambda i, c: (i, 0, 0, 0))
    ya, yb, sc, gc, gs = pl.pallas_call(
        functools.partial(_mixer_kernel, G=G, C=C),
        out_shape=(y_shape, y_shape,
                   jax.ShapeDtypeStruct((nseq, SC_TAPS - 1, D_MODEL), F32),
                   jax.ShapeDtypeStruct((nseq, GDN_TAPS - 1, GDN_QKV), F32),
                   jax.ShapeDtypeStruct((nseq, GDN_HEADS, GDN_DK, GDN_DV), F32)),
        grid=(nb, NC),
        in_specs=[col(0), col(1), col(2), col(1, GDN_QKV), col(6),
                  pl.BlockSpec((G, C, LANES), lambda i, c: (i, c, 0)),
                  seq3((SC_TAPS - 1, D_MODEL)), seq3((GDN_TAPS - 1, GDN_QKV)), state,
                  _resident((SC_TAPS, D_MODEL)), _resident((GDN_TAPS, GDN_QKV)),
                  _resident((1, LANES)), _resident((1, LANES)), _resident((1, GDN_DV))],
        out_specs=(y_spec, y_spec,
                   seq3((SC_TAPS - 1, D_MODEL)), seq3((GDN_TAPS - 1, GDN_QKV)), state),
        scratch_shapes=[pltpu.VMEM((G, D_MODEL // LANES, HIST + C, LANES), F32),
                        pltpu.VMEM((G, GDN_QKV // LANES, HIST + C, LANES), F32),
                        pltpu.VMEM((G, D_MODEL // LANES, C, LANES), F32),
                        pltpu.VMEM((G, GDN_QKV // LANES, C, LANES), F32)],
        compiler_params=pltpu.CompilerParams(
            dimension_semantics=("arbitrary", "arbitrary"), vmem_limit_bytes=VMEM_LIMIT),
        name=f"mixer_g{G}_c{C}",
    )(proj3, proj3, proj3, proj3, proj3, ab3, sc0, gc0, s0, scw, gcw, alog, dtb, gn)
    return ya.reshape(m, D_MODEL), yb.reshape(m, D_MODEL), sc, gc, gs


def _merge_kernel(ya_ref, yb_ref, ga_ref, gb_ref, x1_ref, wa_ref, wb_ref, wo_ref, x2_ref):
    merged = (_sigmoid(ga_ref[...]) * _dot(ya_ref[...], wa_ref[...])
              + _sigmoid(gb_ref[...]) * _dot(yb_ref[...], wb_ref[...]))
    x2_ref[...] = x1_ref[...] + _dot(merged.astype(BF16), wo_ref[...])


def _merge(ya, yb, proj, x1, wa, wb, wo, *, tm):
    m = x1.shape[0]
    tile = pl.BlockSpec((tm, D_MODEL), lambda i: (i, 0))
    return pl.pallas_call(
        _merge_kernel,
        out_shape=jax.ShapeDtypeStruct((m, D_MODEL), F32),
        grid=(m // tm,),
        in_specs=[tile, tile,
                  pl.BlockSpec((tm, D_MODEL), lambda i: (i, 7)),
                  pl.BlockSpec((tm, D_MODEL), lambda i: (i, 8)),
                  tile, _resident((D_MODEL, D_MODEL)), _resident((D_MODEL, D_MODEL)),
                  _resident((D_MODEL, D_MODEL))],
        out_specs=tile,
        compiler_params=pltpu.CompilerParams(
            dimension_semantics=("arbitrary",), vmem_limit_bytes=VMEM_LIMIT),
        name="merge",
    )(ya, yb, proj, proj, x1, wa, wb, wo)


FFN_TILE = 512
PROJ_TILE = 1024


def _token_tile(m, tm):
    return tm if m % tm == 0 else m


def kernel(x_prompt, x_sample, state_sconv, state_gdn_conv, state_gdn, meta_tokens, norm_ffn1, ffn1_w_gate, ffn1_w_up, ffn1_w_down, norm_mix, w_in, sconv_w, gdn_conv_w, gdn_a_log, gdn_dt_bias, gdn_norm, w_a_out, w_b_out, w_o, norm_ffn2, ffn2_w_gate, ffn2_w_up, ffn2_w_down, norm_final):
    bsz, seq, _ = x_prompt.shape
    dec_b, dec_len, _ = x_sample.shape
    assert norm_ffn1.shape[0] == 1, "single layer"

    wg1, wu1, wd1 = (w[0].astype(BF16) for w in (ffn1_w_gate, ffn1_w_up, ffn1_w_down))
    wg2, wu2, wd2 = (w[0].astype(BF16) for w in (ffn2_w_gate, ffn2_w_up, ffn2_w_down))
    wa, wb, wo = (w[0].astype(BF16) for w in (w_a_out, w_b_out, w_o))
    n_main = 3 * D_MODEL + GDN_QKV
    n_ab = 2 * GDN_HEADS
    w_in0 = w_in[0].astype(BF16)
    w_main = jnp.concatenate([w_in0[:, :n_main], w_in0[:, n_main + n_ab:]], axis=1)
    w_ab = jnp.pad(w_in0[:, n_main:n_main + n_ab], ((0, 0), (0, LANES - n_ab)))
    row = lambda v: v.reshape(1, -1).astype(F32)
    n1, nm, n2, nf = row(norm_ffn1[0]), row(norm_mix[0]), row(norm_ffn2[0]), row(norm_final)
    pad_heads = lambda v: jnp.pad(v.reshape(1, -1).astype(F32), ((0, 0), (0, LANES - GDN_HEADS)))
    alog, dtb = pad_heads(gdn_a_log[0]), pad_heads(gdn_dt_bias[0])
    gn = row(gdn_norm[0])
    scw, gcw = sconv_w[0].astype(F32), gdn_conv_w[0].astype(F32)
    mix_params = (scw, gcw, alog, dtb, gn)

    def front(x):
        m = x.shape[0]
        x1, h = _ffn(x, n1, wg1, wu1, wd1, nm, final=False, tm=_token_tile(m, FFN_TILE))
        proj, ab = _proj(h, w_main, w_ab, tm=_token_tile(m, PROJ_TILE))
        return x1, proj, ab

    def back(ya, yb, proj, x1):
        m = x1.shape[0]
        x2 = _merge(ya, yb, proj, x1, wa, wb, wo, tm=_token_tile(m, PROJ_TILE))
        return _ffn(x2, n2, wg2, wu2, wd2, nf, final=True, tm=_token_tile(m, FFN_TILE))

    _, proj_m, ab_m = front(meta_tokens.astype(F32))
    zeros = lambda *s: jnp.zeros(s, F32)
    _, _, sc_m, gc_m, s_m = _mixer(
        proj_m, ab_m, zeros(1, SC_TAPS - 1, D_MODEL), zeros(1, GDN_TAPS - 1, GDN_QKV),
        zeros(1, GDN_HEADS, GDN_DK, GDN_DV), *mix_params, G=1, C=N_META, NC=1)

    chunk = 64
    x1_p, proj_p, ab_p = front(x_prompt.reshape(bsz * seq, D_MODEL))
    ya_p, yb_p, p_sc, p_gc, p_gs = _mixer(
        proj_p, ab_p,
        jnp.broadcast_to(sc_m, (bsz,) + sc_m.shape[1:]),
        jnp.broadcast_to(gc_m, (bsz,) + gc_m.shape[1:]),
        jnp.broadcast_to(s_m, (bsz,) + s_m.shape[1:]),
        *mix_params, G=2, C=chunk, NC=seq // chunk)
    y_prompt = back(ya_p, yb_p, proj_p, x1_p).reshape(bsz, seq, D_MODEL)

    x1_s, proj_s, ab_s = front(x_sample.reshape(dec_b * dec_len, D_MODEL))
    ya_s, yb_s, s_sc, s_gc, s_gs = _mixer(
        proj_s, ab_s, state_sconv[0], state_gdn_conv[0], state_gdn[0],
        *mix_params, G=16, C=dec_len, NC=1)
    y_sample = back(ya_s, yb_s, proj_s, x1_s).reshape(dec_b, dec_len, D_MODEL)

    return (y_prompt, y_sample, p_sc[None], p_gc[None], p_gs[None],
            s_sc[None], s_gc[None], s_gs[None])
```

```python
import functools

import jax
import jax.numpy as jnp
from jax import lax
from jax.experimental import pallas as pl
from jax.experimental.pallas import tpu as pltpu

D_MODEL = 1024
D_FF = 2816
N_META = 16
SC_TAPS = 3
GDN_HEADS = 8
GDN_DK = 128
GDN_DV = 128
GDN_QKV = 3 * GDN_HEADS * GDN_DK
GDN_TAPS = 4
EPS = 1e-6
L2_EPS = 1e-6

LANES = 128
SUBLANES = 8
FF_CHUNK = 256
VMEM_LIMIT = 52 * 1024 * 1024

F32 = jnp.float32
BF16 = jnp.bfloat16


def _sigmoid(x):
    return 1.0 / (1.0 + jnp.exp(-x))


def _rms(x, g):
    return x * lax.rsqrt(jnp.mean(x * x, axis=-1, keepdims=True) + EPS) * g


def _dot(a, b):
    return jnp.dot(a, b, preferred_element_type=F32)


def _dot_nt(a, b):
    return lax.dot_general(a, b, (((1,), (1,)), ((), ())), preferred_element_type=F32)


def _dot_tn(a, b):
    return lax.dot_general(a, b, (((0,), (0,)), ((), ())), preferred_element_type=F32)


def _split3(x):
    hi = x.astype(BF16)
    r = x - hi.astype(F32)
    mid = r.astype(BF16)
    lo = (r - mid.astype(F32)).astype(BF16)
    return hi, mid, lo


def _dot_exact_lhs(m_bf16, x):
    hi, mid, lo = _split3(x)
    return _dot(m_bf16, hi) + (_dot(m_bf16, mid) + _dot(m_bf16, lo))


def _resident(shape):
    nd = len(shape)
    return pl.BlockSpec(shape, lambda *_: (0,) * nd, pipeline_mode=pl.Buffered(1))


def _ffn_kernel(x_ref, ng_ref, wg_ref, wu_ref, wd_ref, nn_ref, *rest, final):
    if final:
        (y_ref,) = rest
    else:
        x1_ref, h_ref = rest
    x = x_ref[...]
    h = _rms(x, ng_ref[...]).astype(BF16)
    acc = None
    for f in range(D_FF // FF_CHUNK):
        cols = slice(f * FF_CHUNK, (f + 1) * FF_CHUNK)
        g = _dot(h, wg_ref[:, cols])
        u = _dot(h, wu_ref[:, cols])
        a = (g * _sigmoid(g) * u).astype(BF16)
        d = _dot(a, wd_ref[cols, :])
        acc = d if acc is None else acc + d
    x1 = x + 0.5 * acc
    if final:
        y_ref[...] = _rms(x1, nn_ref[...])
    else:
        x1_ref[...] = x1
        h_ref[...] = _rms(x1, nn_ref[...]).astype(BF16)


def _ffn(x, ng, wg, wu, wd, nn, *, final, tm):
    m = x.shape[0]
    tile = pl.BlockSpec((tm, D_MODEL), lambda i: (i, 0))
    if final:
        out_shape = jax.ShapeDtypeStruct((m, D_MODEL), F32)
        out_specs = tile
    else:
        out_shape = (jax.ShapeDtypeStruct((m, D_MODEL), F32),
                     jax.ShapeDtypeStruct((m, D_MODEL), BF16))
        out_specs = (tile, tile)
    return pl.pallas_call(
        functools.partial(_ffn_kernel, final=final),
        out_shape=out_shape,
        grid=(m // tm,),
        in_specs=[tile, _resident((1, D_MODEL)), _resident((D_MODEL, D_FF)),
                  _resident((D_MODEL, D_FF)), _resident((D_FF, D_MODEL)),
                  _resident((1, D_MODEL))],
        out_specs=out_specs,
        compiler_params=pltpu.CompilerParams(
            dimension_semantics=("arbitrary",), vmem_limit_bytes=VMEM_LIMIT),
        name="ffn_final" if final else "ffn_mid",
    )(x, ng, wg, wu, wd, nn)


N_PROJ_BLOCKS = 9
PROJ_STEP_COLS = 3 * D_MODEL


def _proj_kernel(h_ref, w_ref, wab_ref, p_ref, ab_ref):
    j = pl.program_id(1)
    h = h_ref[...]
    off = pl.multiple_of(j * PROJ_STEP_COLS, PROJ_STEP_COLS)
    p_ref[...] = _dot(h, w_ref[:, pl.ds(off, PROJ_STEP_COLS)])

    @pl.when(j == 0)
    def _():
        ab_ref[...] = _dot(h, wab_ref[...])


def _proj(h, w_main, w_ab, *, tm):
    m = h.shape[0]
    n_cols = N_PROJ_BLOCKS * D_MODEL
    return pl.pallas_call(
        _proj_kernel,
        out_shape=(jax.ShapeDtypeStruct((m, n_cols), F32),
                   jax.ShapeDtypeStruct((m, LANES), F32)),
        grid=(m // tm, n_cols // PROJ_STEP_COLS),
        in_specs=[pl.BlockSpec((tm, D_MODEL), lambda i, j: (i, 0)),
                  _resident((D_MODEL, n_cols)),
                  _resident((D_MODEL, LANES))],
        out_specs=(pl.BlockSpec((tm, PROJ_STEP_COLS), lambda i, j: (i, j)),
                   pl.BlockSpec((tm, LANES), lambda i, j: (i, 0))),
        compiler_params=pltpu.CompilerParams(
            dimension_semantics=("arbitrary", "arbitrary"), vmem_limit_bytes=VMEM_LIMIT),
        name="proj",
    )(h, w_main, w_ab)


HIST = SUBLANES


def _ext_store(ext_ref, lo, val):
    for lt in range(ext_ref.shape[1]):
        ext_ref[:, lt, lo:lo + val.shape[1], :] = val[:, :, lt * LANES:(lt + 1) * LANES]


def _ext_load(ext_ref, lo, rows):
    return jnp.concatenate([ext_ref[:, lt, lo:lo + rows, :] for lt in range(ext_ref.shape[1])],
                           axis=-1)


def _causal_conv(ext_ref, w_ref, taps, G, C, out_ref, post=None):
    n_lt = ext_ref.shape[1]
    if C % (SUBLANES * SUBLANES) != 0:
        full = _ext_load(ext_ref, 0, HIST + C)
        outs = []
        for g in range(G):
            acc = full[g, HIST:] * w_ref[taps - 1:taps, :]
            for back in range(1, taps):
                shifted = pltpu.roll(full[g], back, 0)[HIST:]
                acc = acc + shifted * w_ref[taps - 1 - back:taps - back, :]
            outs.append(acc if post is None else post(acc))
        return outs[0] if G == 1 else jnp.concatenate(outs, axis=0)
    n = C // SUBLANES
    for g in range(G):
        for lt in range(n_lt):
            cs = slice(lt * LANES, (lt + 1) * LANES)
            slab = {d: ext_ref[g, lt, pl.ds(HIST + d, n, stride=SUBLANES), :]
                    for d in range(1 - taps, SUBLANES)}
            for a in range(SUBLANES):
                acc = slab[a] * w_ref[taps - 1:taps, cs]
                for back in range(1, taps):
                    acc = acc + slab[a - back] * w_ref[taps - 1 - back:taps - back, cs]
                out_ref[g, lt, pl.ds(a, n, stride=SUBLANES), :] = (
                    acc if post is None else post(acc))
    return _ext_load(out_ref, 0, C).reshape(G * C, n_lt * LANES)


def _mixer_kernel(scb_ref, scc_ref, scx_ref, qkv_ref, z_ref, ab_ref,
                  sc0_ref, gc0_ref, s0_ref,
                  scw_ref, gcw_ref, alog_ref, dtb_ref, gn_ref,
                  ya_ref, yb_ref, sco_ref, gco_ref, so_ref,
                  ext_sc, ext_gc, conv_sc, conv_gc, *, G, C):
    R = G * C
    c = pl.program_id(1)

    @pl.when(c == 0)
    def _():
        _ext_store(ext_sc, HIST - (SC_TAPS - 1), sc0_ref[...])
        _ext_store(ext_gc, HIST - (GDN_TAPS - 1), gc0_ref[...])
        so_ref[...] = s0_ref[...]

    _ext_store(ext_sc, HIST, scc_ref[...] * scx_ref[...])
    conv = _causal_conv(ext_sc, scw_ref, SC_TAPS, G, C, conv_sc)
    y_a = (scb_ref[...].reshape(R, D_MODEL) * conv).astype(BF16)
    ya_ref[...] = y_a.reshape(ya_ref.shape)
    new_sc = _ext_load(ext_sc, HIST + C - (SC_TAPS - 1), SC_TAPS - 1)
    _ext_store(ext_sc, HIST - (SC_TAPS - 1), new_sc)
    sco_ref[...] = new_sc

    _ext_store(ext_gc, HIST, qkv_ref[...])
    qkv = _causal_conv(ext_gc, gcw_ref, GDN_TAPS, G, C, conv_gc, post=lambda x: x * _sigmoid(x))
    new_gc = _ext_load(ext_gc, HIST + C - (GDN_TAPS - 1), GDN_TAPS - 1)
    _ext_store(ext_gc, HIST - (GDN_TAPS - 1), new_gc)
    gco_ref[...] = new_gc

    ab = ab_ref[...].reshape(R, LANES)
    xa = ab + dtb_ref[...]
    softplus = jnp.maximum(xa, 0.0) + jnp.log1p(jnp.exp(-jnp.abs(xa)))
    log_a = -jnp.exp(alog_ref[...]) * softplus
    beta_all = _sigmoid(ab)

    row = lax.broadcasted_iota(jnp.int32, (R, R), 0)
    col = lax.broadcasted_iota(jnp.int32, (R, R), 1)
    if G == 1:
        incl = row >= col
        strict = row > col
        seg_all = jnp.ones((R, R), BF16)
    else:
        same = (row // C) == (col // C)
        incl = jnp.logical_and(same, row >= col)
        strict = jnp.logical_and(same, row > col)
        seg_all = jnp.where(same, 1.0, 0.0).astype(BF16)
    seg_tri = jnp.where(incl, 1.0, 0.0).astype(BF16)
    eye = jnp.where(row == col, 1.0, 0.0).astype(F32)

    g_col = _dot_exact_lhs(seg_tri, log_a)
    g_last = _dot_exact_lhs(seg_all, log_a)
    g_row = g_col.T
    exp_g = jnp.exp(g_col)
    exp_rest = jnp.exp(g_last - g_col)
    exp_last = jnp.exp(g_last)

    n_sq = max(C.bit_length() - 2, 0)

    heads = range(GDN_HEADS)
    n_key = GDN_HEADS * GDN_DK
    q, k, v, be, eg = [], [], [], [], []
    for h in heads:
        q_h = qkv[:, h * GDN_DK:(h + 1) * GDN_DK]
        k_h = qkv[:, n_key + h * GDN_DK:n_key + (h + 1) * GDN_DK]
        q.append(q_h * (lax.rsqrt(jnp.sum(q_h * q_h, axis=-1, keepdims=True) + L2_EPS)
                        * (GDN_DK ** -0.5)))
        k.append(k_h * lax.rsqrt(jnp.sum(k_h * k_h, axis=-1, keepdims=True) + L2_EPS))
        v.append(qkv[:, 2 * n_key + h * GDN_DV:2 * n_key + (h + 1) * GDN_DV])
        be.append(beta_all[:, GDN_HEADS + h:GDN_HEADS + h + 1])
        eg.append(exp_g[:, h:h + 1])

    a_qk, p, t = [], [], []
    for h in heads:
        dec = jnp.exp(jnp.minimum(g_col[:, h:h + 1] - g_row[h:h + 1, :], 0.0))
        k_b = k[h].astype(BF16)
        kk = _dot_nt(k_b, k_b)
        qk = _dot_nt(q[h].astype(BF16), k_b)
        p.append(jnp.where(strict, -(be[h] * kk * dec), 0.0))
        a_qk.append(jnp.where(incl, qk * dec, 0.0).astype(BF16))
        t.append(eye + p[h])

    for _ in range(n_sq):
        for h in heads:
            p_b = p[h].astype(BF16)
            t_b = t[h].astype(BF16)
            p[h] = _dot(p_b, p_b)
            t[h] = t[h] + _dot(t_b, p[h].astype(BF16))

    u, w = [], []
    for h in heads:
        rhs = jnp.concatenate([be[h] * v[h], (be[h] * eg[h]) * k[h]], axis=-1)
        sol = _dot(t[h].astype(BF16), rhs.astype(BF16))
        u.append(sol[:, :GDN_DV])
        w.append(sol[:, GDN_DV:])

    rows = [slice(g * C, (g + 1) * C) for g in range(G)]
    stack = lambda parts: parts[0] if G == 1 else jnp.concatenate(parts, axis=0)
    r = []
    for h in heads:
        q_dec = q[h] * eg[h]
        r.append([_dot(jnp.concatenate([w[h][rs], q_dec[rs]], axis=0).astype(BF16),
                       so_ref[g, h].astype(BF16)) for g, rs in enumerate(rows)])
    nu_b, qs = [], []
    for h in heads:
        nu_b.append(stack([u[h][rs] - r[h][g][:C] for g, rs in enumerate(rows)]).astype(BF16))
        qs.append(stack([r[h][g][C:] for g in range(G)]))
    o = [qs[h] + _dot(a_qk[h], nu_b[h]) for h in heads]
    for h in heads:
        k_dec = (k[h] * exp_rest[:, h:h + 1]).astype(BF16)
        for g, rs in enumerate(rows):
            el = exp_last[g * C:g * C + 1, h:h + 1]
            so_ref[g, h] = el * so_ref[g, h] + _dot_tn(k_dec[rs], nu_b[h][rs])

    for h in heads:
        hs = slice(h * GDN_DV, (h + 1) * GDN_DV)
        on = o[h] * lax.rsqrt(jnp.mean(o[h] * o[h], axis=-1, keepdims=True) + EPS) * gn_ref[...]
        z_h = z_ref[:, :, hs].reshape(R, GDN_DV)
        y_h = (on * (z_h * _sigmoid(z_h))).astype(BF16)
        if len(yb_ref.shape) == 3:
            yb_ref[:, :, hs] = y_h.reshape(G, C, GDN_DV)
        else:
            yb_ref[:, hs] = y_h


def _mixer(proj, ab, sc0, gc0, s0, scw, gcw, alog, dtb, gn, *, G, C, NC):
    R = G * C
    m = proj.shape[0]
    nseq = sc0.shape[0]
    nb = nseq // G
    seq_len = NC * C
    assert m == nseq * seq_len
    proj3 = proj.reshape(nseq, seq_len, proj.shape[1])
    ab3 = ab.reshape(nseq, seq_len, LANES)

    def col(blk, width=D_MODEL):
        return pl.BlockSpec((G, C, width), lambda i, c: (i, c, blk))

    if C % (2 * SUBLANES) == 0:
        y_shape = jax.ShapeDtypeStruct((nseq, seq_len, D_MODEL), BF16)
        y_spec = col(0)
    else:
        assert NC == 1
        y_shape = jax.ShapeDtypeStruct((m, D_MODEL), BF16)
        y_spec = pl.BlockSpec((R, D_MODEL), lambda i, c: (i, 0))
    seq3 = lambda shape: pl.BlockSpec((G,) + shape, lambda i, c: (i, 0, 0))
    state = pl.BlockSpec((G, GDN_HEADS, GDN_DK, GDN_DV), lambda i, c: (i, 0, 0, 0))
    ya, yb, sc, gc, gs = pl.pallas_call(
        functools.partial(_mixer_kernel, G=G, C=C),
        out_shape=(y_shape, y_shape,
                   jax.ShapeDtypeStruct((nseq, SC_TAPS - 1, D_MODEL), F32),
                   jax.ShapeDtypeStruct((nseq, GDN_TAPS - 1, GDN_QKV), F32),
                   jax.ShapeDtypeStruct((nseq, GDN_HEADS, GDN_DK, GDN_DV), F32)),
        grid=(nb, NC),
        in_specs=[col(0), col(1), col(2), col(1, GDN_QKV), col(6),
                  pl.BlockSpec((G, C, LANES), lambda i, c: (i, c, 0)),
                  seq3((SC_TAPS - 1, D_MODEL)), seq3((GDN_TAPS - 1, GDN_QKV)), state,
                  _resident((SC_TAPS, D_MODEL)), _resident((GDN_TAPS, GDN_QKV)),
                  _resident((1, LANES)), _resident((1, LANES)), _resident((1, GDN_DV))],
        out_specs=(y_spec, y_spec,
                   seq3((SC_TAPS - 1, D_MODEL)), seq3((GDN_TAPS - 1, GDN_QKV)), state),
        scratch_shapes=[pltpu.VMEM((G, D_MODEL // LANES, HIST + C, LANES), F32),
                        pltpu.VMEM((G, GDN_QKV // LANES, HIST + C, LANES), F32),
                        pltpu.VMEM((G, D_MODEL // LANES, C, LANES), F32),
                        pltpu.VMEM((G, GDN_QKV // LANES, C, LANES), F32)],
        compiler_params=pltpu.CompilerParams(
            dimension_semantics=("arbitrary", "arbitrary"), vmem_limit_bytes=VMEM_LIMIT),
        name=f"mixer_g{G}_c{C}",
    )(proj3, proj3, proj3, proj3, proj3, ab3, sc0, gc0, s0, scw, gcw, alog, dtb, gn)
    return ya.reshape(m, D_MODEL), yb.reshape(m, D_MODEL), sc, gc, gs


def _merge_kernel(ya_ref, yb_ref, ga_ref, gb_ref, x1_ref, wa_ref, wb_ref, wo_ref, x2_ref):
    merged = (_sigmoid(ga_ref[...]) * _dot(ya_ref[...], wa_ref[...])
              + _sigmoid(gb_ref[...]) * _dot(yb_ref[...], wb_ref[...]))
    x2_ref[...] = x1_ref[...] + _dot(merged.astype(BF16), wo_ref[...])


def _merge(ya, yb, proj, x1, wa, wb, wo, *, tm):
    m = x1.shape[0]
    tile = pl.BlockSpec((tm, D_MODEL), lambda i: (i, 0))
    return pl.pallas_call(
        _merge_kernel,
        out_shape=jax.ShapeDtypeStruct((m, D_MODEL), F32),
        grid=(m // tm,),
        in_specs=[tile, tile,
                  pl.BlockSpec((tm, D_MODEL), lambda i: (i, 7)),
                  pl.BlockSpec((tm, D_MODEL), lambda i: (i, 8)),
                  tile, _resident((D_MODEL, D_MODEL)), _resident((D_MODEL, D_MODEL)),
                  _resident((D_MODEL, D_MODEL))],
        out_specs=tile,
        compiler_params=pltpu.CompilerParams(
            dimension_semantics=("arbitrary",), vmem_limit_bytes=VMEM_LIMIT),
        name="merge",
    )(ya, yb, proj, proj, x1, wa, wb, wo)


FFN_TILE = 512
PROJ_TILE = 1024


def _token_tile(m, tm):
    return tm if m % tm == 0 else m


def kernel(x_prompt, x_sample, state_sconv, state_gdn_conv, state_gdn, meta_tokens, norm_ffn1, ffn1_w_gate, ffn1_w_up, ffn1_w_down, norm_mix, w_in, sconv_w, gdn_conv_w, gdn_a_log, gdn_dt_bias, gdn_norm, w_a_out, w_b_out, w_o, norm_ffn2, ffn2_w_gate, ffn2_w_up, ffn2_w_down, norm_final):
    bsz, seq, _ = x_prompt.shape
    dec_b, dec_len, _ = x_sample.shape
    assert norm_ffn1.shape[0] == 1, "single layer"

    wg1, wu1, wd1 = (w[0].astype(BF16) for w in (ffn1_w_gate, ffn1_w_up, ffn1_w_down))
    wg2, wu2, wd2 = (w[0].astype(BF16) for w in (ffn2_w_gate, ffn2_w_up, ffn2_w_down))
    wa, wb, wo = (w[0].astype(BF16) for w in (w_a_out, w_b_out, w_o))
    n_main = 3 * D_MODEL + GDN_QKV
    n_ab = 2 * GDN_HEADS
    w_in0 = w_in[0].astype(BF16)
    w_main = jnp.concatenate([w_in0[:, :n_main], w_in0[:, n_main + n_ab:]], axis=1)
    w_ab = jnp.pad(w_in0[:, n_main:n_main + n_ab], ((0, 0), (0, LANES - n_ab)))
    row = lambda v: v.reshape(1, -1).astype(F32)
    n1, nm, n2, nf = row(norm_ffn1[0]), row(norm_mix[0]), row(norm_ffn2[0]), row(norm_final)
    pad_heads = lambda v: jnp.pad(v.reshape(1, -1).astype(F32), ((0, 0), (0, LANES - GDN_HEADS)))
    alog, dtb = pad_heads(gdn_a_log[0]), pad_heads(gdn_dt_bias[0])
    gn = row(gdn_norm[0])
    scw, gcw = sconv_w[0].astype(F32), gdn_conv_w[0].astype(F32)
    mix_params = (scw, gcw, alog, dtb, gn)

    def front(x):
        m = x.shape[0]
        x1, h = _ffn(x, n1, wg1, wu1, wd1, nm, final=False, tm=_token_tile(m, FFN_TILE))
        proj, ab = _proj(h, w_main, w_ab, tm=_token_tile(m, PROJ_TILE))
        return x1, proj, ab

    def back(ya, yb, proj, x1):
        m = x1.shape[0]
        x2 = _merge(ya, yb, proj, x1, wa, wb, wo, tm=_token_tile(m, PROJ_TILE))
        return _ffn(x2, n2, wg2, wu2, wd2, nf, final=True, tm=_token_tile(m, FFN_TILE))

    _, proj_m, ab_m = front(meta_tokens.astype(F32))
    zeros = lambda *s: jnp.zeros(s, F32)
    _, _, sc_m, gc_m, s_m = _mixer(
        proj_m, ab_m, zeros(1, SC_TAPS - 1, D_MODEL), zeros(1, GDN_TAPS - 1, GDN_QKV),
        zeros(1, GDN_HEADS, GDN_DK, GDN_DV), *mix_params, G=1, C=N_META, NC=1)

    chunk = 64
    x1_p, proj_p, ab_p = front(x_prompt.reshape(bsz * seq, D_MODEL))
    ya_p, yb_p, p_sc, p_gc, p_gs = _mixer(
        proj_p, ab_p,
        jnp.broadcast_to(sc_m, (bsz,) + sc_m.shape[1:]),
        jnp.broadcast_to(gc_m, (bsz,) + gc_m.shape[1:]),
        jnp.broadcast_to(s_m, (bsz,) + s_m.shape[1:]),
        *mix_params, G=2, C=chunk, NC=seq // chunk)
    y_prompt = back(ya_p, yb_p, proj_p, x1_p).reshape(bsz, seq, D_MODEL)

    x1_s, proj_s, ab_s = front(x_sample.reshape(dec_b * dec_len, D_MODEL))
    ya_s, yb_s, s_sc, s_gc, s_gs = _mixer(
        proj_s, ab_s, state_sconv[0], state_gdn_conv[0], state_gdn[0],
        *mix_params, G=8, C=dec_len, NC=1)
    y_sample = back(ya_s, yb_s, proj_s, x1_s).reshape(dec_b, dec_len, D_MODEL)

    return (y_prompt, y_sample, p_sc[None], p_gc[None], p_gs[None],
            s_sc[None], s_gc[None], s_gs[None])
```

```python
import functools

import jax
import jax.numpy as jnp
from jax import lax
from jax.experimental import pallas as pl
from jax.experimental.pallas import tpu as pltpu

D_MODEL = 1024
D_FF = 2816
N_META = 16
SC_TAPS = 3
GDN_HEADS = 8
GDN_DK = 128
GDN_DV = 128
GDN_QKV = 3 * GDN_HEADS * GDN_DK
GDN_TAPS = 4
EPS = 1e-6
L2_EPS = 1e-6

LANES = 128
SUBLANES = 8
FF_CHUNK = 256
VMEM_LIMIT = 52 * 1024 * 1024

F32 = jnp.float32
BF16 = jnp.bfloat16


def _sigmoid(x):
    return 1.0 / (1.0 + jnp.exp(-x))


def _rms(x, g):
    return x * lax.rsqrt(jnp.mean(x * x, axis=-1, keepdims=True) + EPS) * g


def _dot(a, b):
    return jnp.dot(a, b, preferred_element_type=F32)


def _dot_nt(a, b):
    return lax.dot_general(a, b, (((1,), (1,)), ((), ())), preferred_element_type=F32)


def _dot_tn(a, b):
    return lax.dot_general(a, b, (((0,), (0,)), ((), ())), preferred_element_type=F32)


def _split3(x):
    hi = x.astype(BF16)
    r = x - hi.astype(F32)
    mid = r.astype(BF16)
    lo = (r - mid.astype(F32)).astype(BF16)
    return hi, mid, lo


def _dot_exact_lhs(m_bf16, x):
    hi, mid, lo = _split3(x)
    return _dot(m_bf16, hi) + (_dot(m_bf16, mid) + _dot(m_bf16, lo))


def _resident(shape):
    nd = len(shape)
    return pl.BlockSpec(shape, lambda *_: (0,) * nd, pipeline_mode=pl.Buffered(1))


def _ffn_core(x, ng_ref, wg_ref, wu_ref, wd_ref):
    h = _rms(x, ng_ref[...]).astype(BF16)
    acc = None
    for f in range(D_FF // FF_CHUNK):
        cols = slice(f * FF_CHUNK, (f + 1) * FF_CHUNK)
        g = _dot(h, wg_ref[:, cols])
        u = _dot(h, wu_ref[:, cols])
        a = (g * _sigmoid(g) * u).astype(BF16)
        d = _dot(a, wd_ref[cols, :])
        acc = d if acc is None else acc + d
    return x + 0.5 * acc


def _ffn_kernel(x_ref, ng_ref, wg_ref, wu_ref, wd_ref, nn_ref, *rest, final):
    x1 = _ffn_core(x_ref[...], ng_ref, wg_ref, wu_ref, wd_ref)
    if final:
        (y_ref,) = rest
        y_ref[...] = _rms(x1, nn_ref[...])
    else:
        x1_ref, h_ref = rest
        x1_ref[...] = x1
        h_ref[...] = _rms(x1, nn_ref[...]).astype(BF16)


def _merge_ffn_kernel(ya_ref, yb_ref, ga_ref, gb_ref, x1_ref, wa_ref, wb_ref, wo_ref,
                      ng_ref, wg_ref, wu_ref, wd_ref, nn_ref, y_ref):
    merged = (_sigmoid(ga_ref[...]) * _dot(ya_ref[...], wa_ref[...])
              + _sigmoid(gb_ref[...]) * _dot(yb_ref[...], wb_ref[...]))
    x2 = x1_ref[...] + _dot(merged.astype(BF16), wo_ref[...])
    y_ref[...] = _rms(_ffn_core(x2, ng_ref, wg_ref, wu_ref, wd_ref), nn_ref[...])


def _merge_ffn(ya, yb, proj, x1, wa, wb, wo, ng, wg, wu, wd, nn, *, tm):
    m = x1.shape[0]
    tile = pl.BlockSpec((tm, D_MODEL), lambda i: (i, 0))
    sq = _resident((D_MODEL, D_MODEL))
    return pl.pallas_call(
        _merge_ffn_kernel,
        out_shape=jax.ShapeDtypeStruct((m, D_MODEL), F32),
        grid=(m // tm,),
        in_specs=[tile, tile,
                  pl.BlockSpec((tm, D_MODEL), lambda i: (i, 7)),
                  pl.BlockSpec((tm, D_MODEL), lambda i: (i, 8)),
                  tile, sq, sq, sq,
                  _resident((1, D_MODEL)), _resident((D_MODEL, D_FF)),
                  _resident((D_MODEL, D_FF)), _resident((D_FF, D_MODEL)),
                  _resident((1, D_MODEL))],
        out_specs=tile,
        compiler_params=pltpu.CompilerParams(
            dimension_semantics=("arbitrary",), vmem_limit_bytes=VMEM_LIMIT),
        name="merge_ffn",
    )(ya, yb, proj, proj, x1, wa, wb, wo, ng, wg, wu, wd, nn)


def _ffn(x, ng, wg, wu, wd, nn, *, final, tm):
    m = x.shape[0]
    tile = pl.BlockSpec((tm, D_MODEL), lambda i: (i, 0))
    if final:
        out_shape = jax.ShapeDtypeStruct((m, D_MODEL), F32)
        out_specs = tile
    else:
        out_shape = (jax.ShapeDtypeStruct((m, D_MODEL), F32),
                     jax.ShapeDtypeStruct((m, D_MODEL), BF16))
        out_specs = (tile, tile)
    return pl.pallas_call(
        functools.partial(_ffn_kernel, final=final),
        out_shape=out_shape,
        grid=(m // tm,),
        in_specs=[tile, _resident((1, D_MODEL)), _resident((D_MODEL, D_FF)),
                  _resident((D_MODEL, D_FF)), _resident((D_FF, D_MODEL)),
                  _resident((1, D_MODEL))],
        out_specs=out_specs,
        compiler_params=pltpu.CompilerParams(
            dimension_semantics=("arbitrary",), vmem_limit_bytes=VMEM_LIMIT),
        name="ffn_final" if final else "ffn_mid",
    )(x, ng, wg, wu, wd, nn)


N_PROJ_BLOCKS = 9
PROJ_STEP_COLS = 3 * D_MODEL


def _proj_kernel(h_ref, w_ref, wab_ref, p_ref, ab_ref):
    j = pl.program_id(1)
    h = h_ref[...]
    off = pl.multiple_of(j * PROJ_STEP_COLS, PROJ_STEP_COLS)
    p_ref[...] = _dot(h, w_ref[:, pl.ds(off, PROJ_STEP_COLS)])

    @pl.when(j == 0)
    def _():
        ab_ref[...] = _dot(h, wab_ref[...])


def _proj(h, w_main, w_ab, *, tm):
    m = h.shape[0]
    n_cols = N_PROJ_BLOCKS * D_MODEL
    return pl.pallas_call(
        _proj_kernel,
        out_shape=(jax.ShapeDtypeStruct((m, n_cols), F32),
                   jax.ShapeDtypeStruct((m, LANES), F32)),
        grid=(m // tm, n_cols // PROJ_STEP_COLS),
        in_specs=[pl.BlockSpec((tm, D_MODEL), lambda i, j: (i, 0)),
                  _resident((D_MODEL, n_cols)),
                  _resident((D_MODEL, LANES))],
        out_specs=(pl.BlockSpec((tm, PROJ_STEP_COLS), lambda i, j: (i, j)),
                   pl.BlockSpec((tm, LANES), lambda i, j: (i, 0))),
        compiler_params=pltpu.CompilerParams(
            dimension_semantics=("arbitrary", "arbitrary"), vmem_limit_bytes=VMEM_LIMIT),
        name="proj",
    )(h, w_main, w_ab)


HIST = SUBLANES


def _ext_store(ext_ref, lo, val):
    for lt in range(ext_ref.shape[1]):
        ext_ref[:, lt, lo:lo + val.shape[1], :] = val[:, :, lt * LANES:(lt + 1) * LANES]


def _ext_load(ext_ref, lo, rows):
    return jnp.concatenate([ext_ref[:, lt, lo:lo + rows, :] for lt in range(ext_ref.shape[1])],
                           axis=-1)


def _causal_conv(ext_ref, w_ref, taps, G, C, out_ref, post=None):
    n_lt = ext_ref.shape[1]
    if C % (SUBLANES * SUBLANES) != 0:
        full = _ext_load(ext_ref, 0, HIST + C)
        outs = []
        for g in range(G):
            acc = full[g, HIST:] * w_ref[taps - 1:taps, :]
            for back in range(1, taps):
                shifted = pltpu.roll(full[g], back, 0)[HIST:]
                acc = acc + shifted * w_ref[taps - 1 - back:taps - back, :]
            outs.append(acc if post is None else post(acc))
        return outs[0] if G == 1 else jnp.concatenate(outs, axis=0)
    n = C // SUBLANES
    for g in range(G):
        for lt in range(n_lt):
            cs = slice(lt * LANES, (lt + 1) * LANES)
            slab = {d: ext_ref[g, lt, pl.ds(HIST + d, n, stride=SUBLANES), :]
                    for d in range(1 - taps, SUBLANES)}
            for a in range(SUBLANES):
                acc = slab[a] * w_ref[taps - 1:taps, cs]
                for back in range(1, taps):
                    acc = acc + slab[a - back] * w_ref[taps - 1 - back:taps - back, cs]
                out_ref[g, lt, pl.ds(a, n, stride=SUBLANES), :] = (
                    acc if post is None else post(acc))
    return _ext_load(out_ref, 0, C).reshape(G * C, n_lt * LANES)


def _mixer_kernel(scb_ref, scc_ref, scx_ref, qkv_ref, z_ref, ab_ref,
                  sc0_ref, gc0_ref, s0_ref,
                  scw_ref, gcw_ref, alog_ref, dtb_ref, gn_ref,
                  ya_ref, yb_ref, sco_ref, gco_ref, so_ref,
                  ext_sc, ext_gc, conv_sc, conv_gc, *, G, C):
    R = G * C
    c = pl.program_id(1)

    @pl.when(c == 0)
    def _():
        _ext_store(ext_sc, HIST - (SC_TAPS - 1), sc0_ref[...])
        _ext_store(ext_gc, HIST - (GDN_TAPS - 1), gc0_ref[...])
        so_ref[...] = s0_ref[...]

    _ext_store(ext_sc, HIST, scc_ref[...] * scx_ref[...])
    conv = _causal_conv(ext_sc, scw_ref, SC_TAPS, G, C, conv_sc)
    y_a = (scb_ref[...].reshape(R, D_MODEL) * conv).astype(BF16)
    ya_ref[...] = y_a.reshape(ya_ref.shape)
    new_sc = _ext_load(ext_sc, HIST + C - (SC_TAPS - 1), SC_TAPS - 1)
    _ext_store(ext_sc, HIST - (SC_TAPS - 1), new_sc)
    sco_ref[...] = new_sc

    _ext_store(ext_gc, HIST, qkv_ref[...])
    qkv = _causal_conv(ext_gc, gcw_ref, GDN_TAPS, G, C, conv_gc, post=lambda x: x * _sigmoid(x))
    new_gc = _ext_load(ext_gc, HIST + C - (GDN_TAPS - 1), GDN_TAPS - 1)
    _ext_store(ext_gc, HIST - (GDN_TAPS - 1), new_gc)
    gco_ref[...] = new_gc

    ab = ab_ref[...].reshape(R, LANES)
    xa = ab + dtb_ref[...]
    softplus = jnp.maximum(xa, 0.0) + jnp.log1p(jnp.exp(-jnp.abs(xa)))
    log_a = -jnp.exp(alog_ref[...]) * softplus
    beta_all = _sigmoid(ab)

    row = lax.broadcasted_iota(jnp.int32, (R, R), 0)
    col = lax.broadcasted_iota(jnp.int32, (R, R), 1)
    if G == 1:
        incl = row >= col
        strict = row > col
        seg_all = jnp.ones((R, R), BF16)
    else:
        same = (row // C) == (col // C)
        incl = jnp.logical_and(same, row >= col)
        strict = jnp.logical_and(same, row > col)
        seg_all = jnp.where(same, 1.0, 0.0).astype(BF16)
    seg_tri = jnp.where(incl, 1.0, 0.0).astype(BF16)
    eye = jnp.where(row == col, 1.0, 0.0).astype(F32)

    g_col = _dot_exact_lhs(seg_tri, log_a)
    g_last = _dot_exact_lhs(seg_all, log_a)
    g_row = g_col.T
    exp_g = jnp.exp(g_col)
    exp_rest = jnp.exp(g_last - g_col)
    exp_last = jnp.exp(g_last)

    n_sq = max(C.bit_length() - 2, 0)

    heads = range(GDN_HEADS)
    n_key = GDN_HEADS * GDN_DK
    q, k, v, be, eg = [], [], [], [], []
    for h in heads:
        q_h = qkv[:, h * GDN_DK:(h + 1) * GDN_DK]
        k_h = qkv[:, n_key + h * GDN_DK:n_key + (h + 1) * GDN_DK]
        q.append(q_h * (lax.rsqrt(jnp.sum(q_h * q_h, axis=-1, keepdims=True) + L2_EPS)
                        * (GDN_DK ** -0.5)))
        k.append(k_h * lax.rsqrt(jnp.sum(k_h * k_h, axis=-1, keepdims=True) + L2_EPS))
        v.append(qkv[:, 2 * n_key + h * GDN_DV:2 * n_key + (h + 1) * GDN_DV])
        be.append(beta_all[:, GDN_HEADS + h:GDN_HEADS + h + 1])
        eg.append(exp_g[:, h:h + 1])

    a_qk, p, t = [], [], []
    for h in heads:
        dec = jnp.exp(jnp.minimum(g_col[:, h:h + 1] - g_row[h:h + 1, :], 0.0))
        k_b = k[h].astype(BF16)
        kk = _dot_nt(k_b, k_b)
        qk = _dot_nt(q[h].astype(BF16), k_b)
        p.append(jnp.where(strict, -(be[h] * kk * dec), 0.0))
        a_qk.append(jnp.where(incl, qk * dec, 0.0).astype(BF16))
        t.append(eye + p[h])

    for _ in range(n_sq):
        for h in heads:
            p_b = p[h].astype(BF16)
            t_b = t[h].astype(BF16)
            p[h] = _dot(p_b, p_b)
            t[h] = t[h] + _dot(t_b, p[h].astype(BF16))

    u, w = [], []
    for h in heads:
        rhs = jnp.concatenate([be[h] * v[h], (be[h] * eg[h]) * k[h]], axis=-1)
        sol = _dot(t[h].astype(BF16), rhs.astype(BF16))
        u.append(sol[:, :GDN_DV])
        w.append(sol[:, GDN_DV:])

    rows = [slice(g * C, (g + 1) * C) for g in range(G)]
    stack = lambda parts: parts[0] if G == 1 else jnp.concatenate(parts, axis=0)
    r = []
    for h in heads:
        q_dec = q[h] * eg[h]
        r.append([_dot(jnp.concatenate([w[h][rs], q_dec[rs]], axis=0).astype(BF16),
                       so_ref[g, h].astype(BF16)) for g, rs in enumerate(rows)])
    nu_b, qs = [], []
    for h in heads:
        nu_b.append(stack([u[h][rs] - r[h][g][:C] for g, rs in enumerate(rows)]).astype(BF16))
        qs.append(stack([r[h][g][C:] for g in range(G)]))
    o = [qs[h] + _dot(a_qk[h], nu_b[h]) for h in heads]
    for h in heads:
        k_dec = (k[h] * exp_rest[:, h:h + 1]).astype(BF16)
        for g, rs in enumerate(rows):
            el = exp_last[g * C:g * C + 1, h:h + 1]
            so_ref[g, h] = el * so_ref[g, h] + _dot_tn(k_dec[rs], nu_b[h][rs])

    for h in heads:
        hs = slice(h * GDN_DV, (h + 1) * GDN_DV)
        on = o[h] * lax.rsqrt(jnp.mean(o[h] * o[h], axis=-1, keepdims=True) + EPS) * gn_ref[...]
        z_h = z_ref[:, :, hs].reshape(R, GDN_DV)
        y_h = (on * (z_h * _sigmoid(z_h))).astype(BF16)
        if len(yb_ref.shape) == 3:
            yb_ref[:, :, hs] = y_h.reshape(G, C, GDN_DV)
        else:
            yb_ref[:, hs] = y_h


def _mixer(proj, ab, sc0, gc0, s0, scw, gcw, alog, dtb, gn, *, G, C, NC):
    R = G * C
    m = proj.shape[0]
    nseq = sc0.shape[0]
    nb = nseq // G
    seq_len = NC * C
    assert m == nseq * seq_len
    proj3 = proj.reshape(nseq, seq_len, proj.shape[1])
    ab3 = ab.reshape(nseq, seq_len, LANES)

    def col(blk, width=D_MODEL):
        return pl.BlockSpec((G, C, width), lambda i, c: (i, c, blk))

    if C % (2 * SUBLANES) == 0:
        y_shape = jax.ShapeDtypeStruct((nseq, seq_len, D_MODEL), BF16)
        y_spec = col(0)
    else:
        assert NC == 1
        y_shape = jax.ShapeDtypeStruct((m, D_MODEL), BF16)
        y_spec = pl.BlockSpec((R, D_MODEL), lambda i, c: (i, 0))
    seq3 = lambda shape: pl.BlockSpec((G,) + shape, lambda i, c: (i, 0, 0))
    state = pl.BlockSpec((G, GDN_HEADS, GDN_DK, GDN_DV), lambda i, c: (i, 0, 0, 0))
    ya, yb, sc, gc, gs = pl.pallas_call(
        functools.partial(_mixer_kernel, G=G, C=C),
        out_shape=(y_shape, y_shape,
                   jax.ShapeDtypeStruct((nseq, SC_TAPS - 1, D_MODEL), F32),
                   jax.ShapeDtypeStruct((nseq, GDN_TAPS - 1, GDN_QKV), F32),
                   jax.ShapeDtypeStruct((nseq, GDN_HEADS, GDN_DK, GDN_DV), F32)),
        grid=(nb, NC),
        in_specs=[col(0), col(1), col(2), col(1, GDN_QKV), col(6),
                  pl.BlockSpec((G, C, LANES), lambda i, c: (i, c, 0)),
                  seq3((SC_TAPS - 1, D_MODEL)), seq3((GDN_TAPS - 1, GDN_QKV)), state,
                  _resident((SC_TAPS, D_MODEL)), _resident((GDN_TAPS, GDN_QKV)),
                  _resident((1, LANES)), _resident((1, LANES)), _resident((1, GDN_DV))],
        out_specs=(y_spec, y_spec,
                   seq3((SC_TAPS - 1, D_MODEL)), seq3((GDN_TAPS - 1, GDN_QKV)), state),
        scratch_shapes=[pltpu.VMEM((G, D_MODEL // LANES, HIST + C, LANES), F32),
                        pltpu.VMEM((G, GDN_QKV // LANES, HIST + C, LANES), F32),
                        pltpu.VMEM((G, D_MODEL // LANES, C, LANES), F32),
                        pltpu.VMEM((G, GDN_QKV // LANES, C, LANES), F32)],
        compiler_params=pltpu.CompilerParams(
            dimension_semantics=("arbitrary", "arbitrary"), vmem_limit_bytes=VMEM_LIMIT),
        name=f"mixer_g{G}_c{C}",
    )(proj3, proj3, proj3, proj3, proj3, ab3, sc0, gc0, s0, scw, gcw, alog, dtb, gn)
    return ya.reshape(m, D_MODEL), yb.reshape(m, D_MODEL), sc, gc, gs


def _merge_kernel(ya_ref, yb_ref, ga_ref, gb_ref, x1_ref, wa_ref, wb_ref, wo_ref, x2_ref):
    merged = (_sigmoid(ga_ref[...]) * _dot(ya_ref[...], wa_ref[...])
              + _sigmoid(gb_ref[...]) * _dot(yb_ref[...], wb_ref[...]))
    x2_ref[...] = x1_ref[...] + _dot(merged.astype(BF16), wo_ref[...])


def _merge(ya, yb, proj, x1, wa, wb, wo, *, tm):
    m = x1.shape[0]
    tile = pl.BlockSpec((tm, D_MODEL), lambda i: (i, 0))
    return pl.pallas_call(
        _merge_kernel,
        out_shape=jax.ShapeDtypeStruct((m, D_MODEL), F32),
        grid=(m // tm,),
        in_specs=[tile, tile,
                  pl.BlockSpec((tm, D_MODEL), lambda i: (i, 7)),
                  pl.BlockSpec((tm, D_MODEL), lambda i: (i, 8)),
                  tile, _resident((D_MODEL, D_MODEL)), _resident((D_MODEL, D_MODEL)),
                  _resident((D_MODEL, D_MODEL))],
        out_specs=tile,
        compiler_params=pltpu.CompilerParams(
            dimension_semantics=("arbitrary",), vmem_limit_bytes=VMEM_LIMIT),
        name="merge",
    )(ya, yb, proj, proj, x1, wa, wb, wo)


FFN_TILE = 512
PROJ_TILE = 1024


def _token_tile(m, tm):
    return tm if m % tm == 0 else m


def kernel(x_prompt, x_sample, state_sconv, state_gdn_conv, state_gdn, meta_tokens, norm_ffn1, ffn1_w_gate, ffn1_w_up, ffn1_w_down, norm_mix, w_in, sconv_w, gdn_conv_w, gdn_a_log, gdn_dt_bias, gdn_norm, w_a_out, w_b_out, w_o, norm_ffn2, ffn2_w_gate, ffn2_w_up, ffn2_w_down, norm_final):
    bsz, seq, _ = x_prompt.shape
    dec_b, dec_len, _ = x_sample.shape
    assert norm_ffn1.shape[0] == 1, "single layer"

    wg1, wu1, wd1 = (w[0].astype(BF16) for w in (ffn1_w_gate, ffn1_w_up, ffn1_w_down))
    wg2, wu2, wd2 = (w[0].astype(BF16) for w in (ffn2_w_gate, ffn2_w_up, ffn2_w_down))
    wa, wb, wo = (w[0].astype(BF16) for w in (w_a_out, w_b_out, w_o))
    n_main = 3 * D_MODEL + GDN_QKV
    n_ab = 2 * GDN_HEADS
    w_in0 = w_in[0].astype(BF16)
    w_main = jnp.concatenate([w_in0[:, :n_main], w_in0[:, n_main + n_ab:]], axis=1)
    w_ab = jnp.pad(w_in0[:, n_main:n_main + n_ab], ((0, 0), (0, LANES - n_ab)))
    row = lambda v: v.reshape(1, -1).astype(F32)
    n1, nm, n2, nf = row(norm_ffn1[0]), row(norm_mix[0]), row(norm_ffn2[0]), row(norm_final)
    pad_heads = lambda v: jnp.pad(v.reshape(1, -1).astype(F32), ((0, 0), (0, LANES - GDN_HEADS)))
    alog, dtb = pad_heads(gdn_a_log[0]), pad_heads(gdn_dt_bias[0])
    gn = row(gdn_norm[0])
    scw, gcw = sconv_w[0].astype(F32), gdn_conv_w[0].astype(F32)
    mix_params = (scw, gcw, alog, dtb, gn)

    def front(x):
        m = x.shape[0]
        x1, h = _ffn(x, n1, wg1, wu1, wd1, nm, final=False, tm=_token_tile(m, FFN_TILE))
        proj, ab = _proj(h, w_main, w_ab, tm=_token_tile(m, PROJ_TILE))
        return x1, proj, ab

    def back(ya, yb, proj, x1):
        return _merge_ffn(ya, yb, proj, x1, wa, wb, wo, n2, wg2, wu2, wd2, nf,
                          tm=_token_tile(x1.shape[0], FFN_TILE))

    _, proj_m, ab_m = front(meta_tokens.astype(F32))
    zeros = lambda *s: jnp.zeros(s, F32)
    _, _, sc_m, gc_m, s_m = _mixer(
        proj_m, ab_m, zeros(1, SC_TAPS - 1, D_MODEL), zeros(1, GDN_TAPS - 1, GDN_QKV),
        zeros(1, GDN_HEADS, GDN_DK, GDN_DV), *mix_params, G=1, C=N_META, NC=1)

    chunk = 64
    x1_p, proj_p, ab_p = front(x_prompt.reshape(bsz * seq, D_MODEL))
    ya_p, yb_p, p_sc, p_gc, p_gs = _mixer(
        proj_p, ab_p,
        jnp.broadcast_to(sc_m, (bsz,) + sc_m.shape[1:]),
        jnp.broadcast_to(gc_m, (bsz,) + gc_m.shape[1:]),
        jnp.broadcast_to(s_m, (bsz,) + s_m.shape[1:]),
        *mix_params, G=2, C=chunk, NC=seq // chunk)
    y_prompt = back(ya_p, yb_p, proj_p, x1_p).reshape(bsz, seq, D_MODEL)

    x1_s, proj_s, ab_s = front(x_sample.reshape(dec_b * dec_len, D_MODEL))
    ya_s, yb_s, s_sc, s_gc, s_gs = _mixer(
        proj_s, ab_s, state_sconv[0], state_gdn_conv[0], state_gdn[0],
        *mix_params, G=8, C=dec_len, NC=1)
    y_sample = back(ya_s, yb_s, proj_s, x1_s).reshape(dec_b, dec_len, D_MODEL)

    return (y_prompt, y_sample, p_sc[None], p_gc[None], p_gs[None],
            s_sc[None], s_gc[None], s_gs[None])
```
